```python
import math
import jax, jax.numpy as jnp
from jax import lax
import numpy as np

D_MODEL = 4096
BATCH = 4
SEQ = 2048
DEPTH = 2
DEC_BATCH = 8
DEC_SEQ = 1
PAST_LEN = 16384
PAGE_SIZE = 128

MIX_WIDTH = D_MODEL
ATT_WIDTH = MIX_WIDTH // 2
HGRN_WIDTH = MIX_WIDTH - ATT_WIDTH
ATT_HEAD_V = 128
ATT_HEADS = ATT_WIDTH // ATT_HEAD_V
ATT_HEAD_QK = ATT_HEAD_V // 2
ATT_SCALE = ATT_HEAD_QK ** -0.5
ROT_DIMS = ATT_HEAD_QK // 4
ROPE_THETA = 500000.0
Q_BLOCK = 128
HGRN_HEADS = 16
HGRN_I = HGRN_WIDTH // HGRN_HEADS
HGRN_F = 128
HGRN_CHUNK = 64
D_FF = ((8 * D_MODEL // 3 + 255) // 256) * 256
ALPHA = (2 * DEPTH) ** 0.25
BETA = (8 * DEPTH) ** -0.25
EPS = 1e-5
COLS_QA = ATT_HEADS * 2 * ATT_HEAD_QK
COLS_KA = ATT_HEADS * 2 * ATT_HEAD_QK
COLS_VA = ATT_HEADS * ATT_HEAD_V
COLS_QH = HGRN_HEADS * HGRN_F
COLS_FH = HGRN_HEADS * HGRN_F
COLS_IH = HGRN_WIDTH
COLS_GH = HGRN_WIDTH
IN_COLS = COLS_QA + COLS_KA + COLS_VA + COLS_QH + COLS_FH + COLS_IH + COLS_GH
IN_SPLITS = tuple(int(c) for c in np.cumsum([COLS_QA, COLS_KA, COLS_VA, COLS_QH, COLS_FH, COLS_IH]))

kernel_name = 'hymba_hgrn2_diffattn_macaron_deepnorm'


def layer_norm(x, g, b):
    xf = x.astype(jnp.float32)
    mu = jnp.mean(xf, axis=-1, keepdims=True)
    xc = xf - mu
    var = jnp.mean(xc * xc, axis=-1, keepdims=True)
    return (xc * lax.rsqrt(var + EPS) * g.astype(jnp.float32) + b.astype(jnp.float32)).astype(x.dtype)


def rms_norm(x, g):
    xf = x.astype(jnp.float32)
    return xf * lax.rsqrt(jnp.mean(xf * xf, axis=-1, keepdims=True) + EPS) * g.astype(jnp.float32)


def swiglu(x, wg, wu, wd):
    return (jax.nn.silu(x @ wg) * (x @ wu)) @ wd


def rotary(x, pos):
    half = ROT_DIMS // 2
    inv_freq = jnp.exp(-math.log(ROPE_THETA) * jnp.arange(half, dtype=jnp.float32) / half)
    ang = pos.astype(jnp.float32)[:, None] * inv_freq[None, :]
    cos = jnp.cos(ang)[None, :, None, None, :]
    sin = jnp.sin(ang)[None, :, None, None, :]
    xr = x[..., :ROT_DIMS].astype(jnp.float32)
    x1, x2 = xr[..., :half], xr[..., half:]
    rot = jnp.concatenate([x1 * cos - x2 * sin, x2 * cos + x1 * sin], axis=-1)
    return jnp.concatenate([rot.astype(x.dtype), x[..., ROT_DIMS:]], axis=-1)


def diff_attend(q, q_pos, k_segs, v_segs, pos_segs, lam):
    scores = []
    for k, kp in zip(k_segs, pos_segs):
        s = jnp.einsum('bqhcd,bkhcd->bhcqk', q, k, preferred_element_type=jnp.float32) * ATT_SCALE
        scores.append(jnp.where(kp[None, :] <= q_pos[:, None], s, -jnp.inf))
    p = jax.nn.softmax(jnp.concatenate(scores, axis=-1), axis=-1)
    a = p[:, :, 0] - lam * p[:, :, 1]
    outs = []
    start = 0
    for v in v_segs:
        n = v.shape[1]
        outs.append(jnp.einsum('bhqk,bkhd->bqhd', a[..., start:start + n], v,
                               preferred_element_type=jnp.float32))
        start += n
    out = outs[0]
    for o in outs[1:]:
        out = out + o
    return out


def prompt_attention(q, pos, k, v, lam):
    B, T = q.shape[:2]
    if T % Q_BLOCK:
        return diff_attend(q, pos, (k,), (v,), (pos,), lam)
    nb = T // Q_BLOCK
    qb = jnp.moveaxis(q.reshape(B, nb, Q_BLOCK, ATT_HEADS, 2, ATT_HEAD_QK), 1, 0)
    pb = pos.reshape(nb, Q_BLOCK)
    out = lax.map(lambda qp: diff_attend(qp[0], qp[1], (k,), (v,), (pos,), lam), (qb, pb))
    return jnp.moveaxis(out, 0, 1).reshape(B, T, ATT_HEADS, ATT_HEAD_V)


def gla_chunked(q, k, v, g, s0):
    B, T, H, F = q.shape
    I = v.shape[-1]
    c = math.gcd(T, HGRN_CHUNK)
    n = T // c

    def to_chunks(t):
        return t.reshape(B, n, c, H, t.shape[-1]).transpose(1, 0, 3, 2, 4)

    causal = jnp.tril(jnp.ones((c, c), dtype=bool))[:, :, None]

    def step(state, blk):
        qb, kb, vb, gb = blk
        qb = qb.astype(jnp.float32)
        kb = kb.astype(jnp.float32)
        vb = vb.astype(jnp.float32)
        b = jnp.cumsum(gb.astype(jnp.float32), axis=2)
        o_inter = jnp.einsum('bhtf,bhfi->bhti', qb * jnp.exp(b), state)
        diff = b[:, :, :, None, :] - b[:, :, None, :, :]
        decay = jnp.exp(jnp.where(causal, diff, -jnp.inf))
        attn = jnp.einsum('bhtf,bhsf,bhtsf->bhts', qb, kb, decay)
        o = o_inter + jnp.einsum('bhts,bhsi->bhti', attn, vb)
        b_last = b[:, :, -1:, :]
        k_dec = kb * jnp.exp(b_last - b)
        state = jnp.exp(b_last[:, :, 0, :])[..., None] * state + jnp.einsum('bhsf,bhsi->bhfi', k_dec, vb)
        return state, o

    s_fin, o = lax.scan(step, s0.astype(jnp.float32), (to_chunks(q), to_chunks(k), to_chunks(v), to_chunks(g)))
    o = o.transpose(1, 0, 3, 2, 4).reshape(B, T, H, I)
    return o, s_fin


def hgrn_lower_bounds(raw):
    sm = jax.nn.softmax(raw.astype(jnp.float32), axis=0)
    return jnp.cumsum(sm, axis=0) - sm[0:1]


def token_mixer(h, pos, s0, past, w_in_l, lam_qk_l, subln_l, lb_l, gnorm_l, w_out_l, lambda_init):
    B, T, _ = h.shape
    proj = h @ w_in_l
    qa, ka, va, qh, fh, ih, gh = jnp.split(proj, IN_SPLITS, axis=-1)

    qa = rotary(qa.reshape(B, T, ATT_HEADS, 2, ATT_HEAD_QK), pos)
    ka = rotary(ka.reshape(B, T, ATT_HEADS, 2, ATT_HEAD_QK), pos)
    va = va.reshape(B, T, ATT_HEADS, ATT_HEAD_V)
    lq = lam_qk_l.astype(jnp.float32)
    lam = jnp.exp(jnp.sum(lq[0] * lq[1])) - jnp.exp(jnp.sum(lq[2] * lq[3])) + lambda_init
    if past is None:
        att = prompt_attention(qa, pos, ka, va, lam)
    else:
        k_past, v_past, past_pos = past
        att = diff_attend(qa, pos, (k_past, ka), (v_past, va), (past_pos, pos), lam)
    att = rms_norm(att, subln_l) * (1.0 - lambda_init)

    lb = lb_l.reshape(HGRN_HEADS, HGRN_F)
    f_gate = lb + (1.0 - lb) * jax.nn.sigmoid(fh.astype(jnp.float32).reshape(B, T, HGRN_HEADS, HGRN_F))
    q_h = jax.nn.silu(qh.astype(jnp.float32).reshape(B, T, HGRN_HEADS, HGRN_F))
    k_h = 1.0 - f_gate
    g_log = jnp.log(f_gate)
    v_h = ih.reshape(B, T, HGRN_HEADS, HGRN_I)
    o_h, s_new = gla_chunked(q_h, k_h, v_h, g_log, s0)
    o_h = rms_norm(o_h, gnorm_l) * jax.nn.silu(gh.astype(jnp.float32).reshape(B, T, HGRN_HEADS, HGRN_I))

    merged = jnp.concatenate([att.reshape(B, T, ATT_WIDTH).astype(h.dtype),
                              o_h.reshape(B, T, HGRN_WIDTH).astype(h.dtype)], axis=-1)
    return merged @ w_out_l, ka.reshape(B, T, ATT_HEADS, 2 * ATT_HEAD_QK), va, s_new


def run_layers(x, pos, state0, cache_k, cache_v, page_table, ln_gain, ln_bias, ffn1_gate, ffn1_up,
               ffn1_down, w_in, lambda_qk, subln_gain, lb_all, hgrn_gnorm_gain, w_out, ffn2_gate,
               ffn2_up, ffn2_down):
    B, T, _ = x.shape
    new_k, new_v, new_s = [], [], []
    for l in range(DEPTH):
        lambda_init = 0.8 - 0.6 * math.exp(-0.3 * l)
        x = layer_norm(ALPHA * x + 0.5 * swiglu(x, ffn1_gate[l], ffn1_up[l], ffn1_down[l]),
                       ln_gain[l, 0], ln_bias[l, 0])
        if cache_k is None:
            s0 = jnp.zeros((B, HGRN_HEADS, HGRN_F, HGRN_I), jnp.float32)
            past = None
        else:
            s0 = state0[l]
            n_past = page_table.shape[1] * PAGE_SIZE
            k_past = cache_k[l, page_table].reshape(B, n_past, ATT_HEADS, 2, ATT_HEAD_QK)
            v_past = cache_v[l, page_table].reshape(B, n_past, ATT_HEADS, ATT_HEAD_V)
            past = (k_past, v_past, jnp.arange(n_past, dtype=jnp.int32))
        mixed, k_l, v_l, s_l = token_mixer(x, pos, s0, past, w_in[l], lambda_qk[l], subln_gain[l],
                                           lb_all[l], hgrn_gnorm_gain[l], w_out[l], lambda_init)
        x = layer_norm(ALPHA * x + mixed, ln_gain[l, 1], ln_bias[l, 1])
        x = layer_norm(ALPHA * x + 0.5 * swiglu(x, ffn2_gate[l], ffn2_up[l], ffn2_down[l]),
                       ln_gain[l, 2], ln_bias[l, 2])
        new_k.append(k_l.astype(x.dtype))
        new_v.append(v_l.astype(x.dtype))
        new_s.append(s_l.astype(x.dtype))
    return x, jnp.stack(new_k), jnp.stack(new_v), jnp.stack(new_s)


def setup_inputs(seed: int = 0) -> dict:
    key = jax.random.key(seed)
    ks = jax.random.split(key, 20)
    f32 = jnp.float32
    n_pages = PAST_LEN // PAGE_SIZE
    n_used = DEC_BATCH * n_pages
    n_pool = n_used + n_used // 4
    col_scale = jnp.concatenate([
        jnp.ones((COLS_QA + COLS_KA,), f32), jnp.full((COLS_VA,), BETA, f32),
        jnp.ones((COLS_QH + COLS_FH,), f32), jnp.full((COLS_IH,), BETA, f32),
        jnp.ones((COLS_GH,), f32)])

    def nrm(k, shape, scale):
        return jax.random.normal(k, shape, f32) * scale

    return {
        'x_prompt': nrm(ks[0], (BATCH, SEQ, D_MODEL), 1.0),
        'x_sample': nrm(ks[1], (DEC_BATCH, DEC_SEQ, D_MODEL), 1.0),
        'cache_k': nrm(ks[2], (DEPTH, n_pool, PAGE_SIZE, ATT_HEADS, 2 * ATT_HEAD_QK), 1.0),
        'cache_v': nrm(ks[3], (DEPTH, n_pool, PAGE_SIZE, ATT_HEADS, ATT_HEAD_V), 1.0),
        'state_hgrn': nrm(ks[4], (DEPTH, DEC_BATCH, HGRN_HEADS, HGRN_F, HGRN_I), 0.5),
        'page_table': jax.random.permutation(ks[5], n_pool)[:n_used].reshape(DEC_BATCH, n_pages).astype(jnp.int32),
        'ln_gain': 1.0 + nrm(ks[6], (DEPTH, 3, D_MODEL), 0.01),
        'ln_bias': nrm(ks[7], (DEPTH, 3, D_MODEL), 0.01),
        'ffn1_gate': nrm(ks[8], (DEPTH, D_MODEL, D_FF), D_MODEL ** -0.5),
        'ffn1_up': nrm(ks[9], (DEPTH, D_MODEL, D_FF), D_MODEL ** -0.5),
        'ffn1_down': nrm(ks[10], (DEPTH, D_FF, D_MODEL), BETA * D_FF ** -0.5),
        'w_in': nrm(ks[11], (DEPTH, D_MODEL, IN_COLS), D_MODEL ** -0.5) * col_scale,
        'lambda_qk': nrm(ks[12], (DEPTH, 4, ATT_HEAD_QK), 0.1),
        'subln_gain': 1.0 + nrm(ks[13], (DEPTH, ATT_HEAD_V), 0.01),
        'hgrn_lower_bound': nrm(ks[14], (DEPTH, COLS_FH), 0.1),
        'hgrn_gnorm_gain': 1.0 + nrm(ks[15], (DEPTH, HGRN_I), 0.01),
        'w_out': nrm(ks[16], (DEPTH, MIX_WIDTH, D_MODEL), BETA * MIX_WIDTH ** -0.5),
        'ffn2_gate': nrm(ks[17], (DEPTH, D_MODEL, D_FF), D_MODEL ** -0.5),
        'ffn2_up': nrm(ks[18], (DEPTH, D_MODEL, D_FF), D_MODEL ** -0.5),
        'ffn2_down': nrm(ks[19], (DEPTH, D_FF, D_MODEL), BETA * D_FF ** -0.5),
    }


def reference(x_prompt, x_sample, cache_k, cache_v, state_hgrn, page_table, ln_gain, ln_bias,
              ffn1_gate, ffn1_up, ffn1_down, w_in, lambda_qk, subln_gain, hgrn_lower_bound,
              hgrn_gnorm_gain, w_out, ffn2_gate, ffn2_up, ffn2_down):
    lb_all = hgrn_lower_bounds(hgrn_lower_bound)
    pos_p = jnp.arange(x_prompt.shape[1], dtype=jnp.int32)
    y_prompt, k_prompt, v_prompt, state_prompt = run_layers(
        x_prompt, pos_p, None, None, None, None, ln_gain, ln_bias, ffn1_gate, ffn1_up, ffn1_down,
        w_in, lambda_qk, subln_gain, lb_all, hgrn_gnorm_gain, w_out, ffn2_gate, ffn2_up, ffn2_down)
    n_past = page_table.shape[1] * PAGE_SIZE
    pos_s = n_past + jnp.arange(x_sample.shape[1], dtype=jnp.int32)
    y_sample, k_sample, v_sample, state_sample = run_layers(
        x_sample, pos_s, state_hgrn, cache_k, cache_v, page_table, ln_gain, ln_bias, ffn1_gate,
        ffn1_up, ffn1_down, w_in, lambda_qk, subln_gain, lb_all, hgrn_gnorm_gain, w_out, ffn2_gate,
        ffn2_up, ffn2_down)
    return (y_prompt, y_sample, k_prompt, v_prompt, state_prompt, k_sample, v_sample, state_sample)
```

```python
import functools
import math

import numpy as np
import jax
import jax.numpy as jnp
from jax import lax
from jax.experimental import pallas as pl
from jax.experimental.pallas import tpu as pltpu

_F32 = jnp.float32
_BF16 = jnp.bfloat16

EPS = 1e-5
ROPE_THETA = 500000.0
HGRN_CHUNK = 64
V7X_VMEM_BYTES = 64 * 1024 * 1024
LANES = 128


def _compiler_params(semantics, block_bytes):
    limit = min(int(block_bytes * 1.25) + (8 << 20), V7X_VMEM_BYTES - (6 << 20))
    return pltpu.CompilerParams(dimension_semantics=semantics, vmem_limit_bytes=limit)


def _nbytes(shape, dtype):
    return int(np.prod(shape)) * jnp.dtype(dtype).itemsize


def _mm_body(x_ref, w_ref, o_ref):
    o_ref[...] = jnp.dot(x_ref[...], w_ref[...], preferred_element_type=_F32).astype(o_ref.dtype)


def _matmul(x, w, layer, *, tm, tn, out_dtype=_F32, name):
    M, K = x.shape
    N = w.shape[2]
    assert M % tm == 0 and N % tn == 0
    blocks = 2 * (_nbytes((tm, K), x.dtype) + _nbytes((K, tn), w.dtype) + _nbytes((tm, tn), out_dtype))
    return pl.pallas_call(
        _mm_body,
        grid=(M // tm, N // tn),
        in_specs=[pl.BlockSpec((tm, K), lambda i, j: (i, 0)),
                  pl.BlockSpec((None, K, tn), lambda i, j: (layer, 0, j))],
        out_specs=pl.BlockSpec((tm, tn), lambda i, j: (i, j)),
        out_shape=jax.ShapeDtypeStruct((M, N), out_dtype),
        compiler_params=_compiler_params(("parallel", "arbitrary"), blocks + _nbytes((tm, tn), _F32)),
        name=name,
    )(x, w)


def _ffn_up_body(x_ref, wg_ref, wu_ref, o_ref):
    x = x_ref[...]
    g = jnp.dot(x, wg_ref[...], preferred_element_type=_F32)
    u = jnp.dot(x, wu_ref[...], preferred_element_type=_F32)
    o_ref[...] = (g / (1.0 + jnp.exp(-g)) * u).astype(o_ref.dtype)


def _ffn_up(x, wg, wu, layer, *, tm, tn, name):
    M, K = x.shape
    N = wg.shape[2]
    assert M % tm == 0 and N % tn == 0
    blocks = 2 * (_nbytes((tm, K), x.dtype) + 2 * _nbytes((K, tn), wg.dtype) + _nbytes((tm, tn), _BF16))
    return pl.pallas_call(
        _ffn_up_body,
        grid=(M // tm, N // tn),
        in_specs=[pl.BlockSpec((tm, K), lambda i, j: (i, 0)),
                  pl.BlockSpec((None, K, tn), lambda i, j: (layer, 0, j)),
                  pl.BlockSpec((None, K, tn), lambda i, j: (layer, 0, j))],
        out_specs=pl.BlockSpec((tm, tn), lambda i, j: (i, j)),
        out_shape=jax.ShapeDtypeStruct((M, N), _BF16),
        compiler_params=_compiler_params(("parallel", "arbitrary"), blocks + 3 * _nbytes((tm, tn), _F32)),
        name=name,
    )(x, wg, wu)


def _ln_body(x_ref, a_ref, g_ref, b_ref, y_ref, yb_ref, *, alpha, scale):
    z = alpha * x_ref[...] + scale * a_ref[...]
    mu = jnp.mean(z, axis=-1, keepdims=True)
    zc = z - mu
    var = jnp.mean(zc * zc, axis=-1, keepdims=True)
    y = zc * lax.rsqrt(var + EPS) * g_ref[...] + b_ref[...]
    y_ref[...] = y
    yb_ref[...] = y.astype(_BF16)


def _residual_ln(x, a, gain, bias, *, alpha, scale, tr, name):
    M, D = x.shape
    assert M % tr == 0
    blocks = 2 * (3 * _nbytes((tr, D), _F32) + _nbytes((tr, D), _BF16))
    row = pl.BlockSpec((tr, D), lambda i: (i, 0))
    vec = pl.BlockSpec((1, D), lambda i: (0, 0))
    return pl.pallas_call(
        functools.partial(_ln_body, alpha=alpha, scale=scale),
        grid=(M // tr,),
        in_specs=[row, row, vec, vec],
        out_specs=[row, row],
        out_shape=[jax.ShapeDtypeStruct((M, D), _F32), jax.ShapeDtypeStruct((M, D), _BF16)],
        compiler_params=_compiler_params(("parallel",), blocks + 2 * _nbytes((tr, D), _F32)),
        name=name,
    )(x, a, gain.reshape(1, D), bias.reshape(1, D))


def _rotary_tables(pos, dk, rot):
    half = rot // 2
    inv_freq = jnp.exp(-math.log(ROPE_THETA) * jnp.arange(half, dtype=_F32) / half)
    ang = pos.astype(_F32)[:, None] * inv_freq[None, :]
    cos, sin = jnp.cos(ang), jnp.sin(ang)
    n = pos.shape[0]
    ones = jnp.ones((n, dk - rot), _F32)
    zeros_h = jnp.zeros((n, half), _F32)
    zeros_r = jnp.zeros((n, dk - rot), _F32)
    c = jnp.concatenate([cos, cos, ones], axis=1)
    s_hi = jnp.concatenate([-sin, zeros_h, zeros_r], axis=1)
    s_lo = jnp.concatenate([zeros_h, sin, zeros_r], axis=1)
    return tuple(jnp.concatenate([t, t], axis=1) for t in (c, s_hi, s_lo))


def _rotary_body(q_ref, k_ref, c_ref, shi_ref, slo_ref, qo_ref, ko_ref, *, heads, width, half, q_scale):
    c, s_hi, s_lo = c_ref[...], shi_ref[...], slo_ref[...]
    for h in range(heads):
        cols = slice(h * width, (h + 1) * width)
        for src, dst, scale in ((q_ref, qo_ref, q_scale), (k_ref, ko_ref, None)):
            x = src[:, cols]
            y = x * c + pltpu.roll(x, width - half, 1) * s_hi + pltpu.roll(x, half, 1) * s_lo
            if scale is not None:
                y = y * scale
            dst[:, cols] = y.astype(dst.dtype)


def _rotary(proj, tables, *, heads, width, half, q_scale, tr, q_dtype, name):
    M = proj.shape[0]
    W = heads * width
    n_tab = tables[0].shape[0] // tr
    blocks = 2 * (2 * _nbytes((tr, W), _F32) + 3 * _nbytes((tr, width), _F32)
                  + _nbytes((tr, W), q_dtype) + _nbytes((tr, W), _F32))
    tab = pl.BlockSpec((tr, width), lambda i: (i % n_tab, 0))
    return pl.pallas_call(
        functools.partial(_rotary_body, heads=heads, width=width, half=half, q_scale=q_scale),
        grid=(M // tr,),
        in_specs=[pl.BlockSpec((tr, W), lambda i: (i, 0)), pl.BlockSpec((tr, W), lambda i: (i, 1)),
                  tab, tab, tab],
        out_specs=[pl.BlockSpec((tr, W), lambda i: (i, 0)), pl.BlockSpec((tr, W), lambda i: (i, 0))],
        out_shape=[jax.ShapeDtypeStruct((M, W), q_dtype), jax.ShapeDtypeStruct((M, W), _F32)],
        compiler_params=_compiler_params(("parallel",), blocks),
        name=name,
    )(proj, proj, *tables)


def _lambda_value(lq, lambda_init):
    s01 = jnp.sum(lq[0:1, :] * lq[1:2, :], axis=1, keepdims=True)
    s23 = jnp.sum(lq[2:3, :] * lq[3:4, :], axis=1, keepdims=True)
    return jnp.exp(s01) - jnp.exp(s23) + lambda_init


def _sub_rms(x, gain, lambda_init):
    ms = jnp.mean(x * x, axis=-1, keepdims=True)
    return x * lax.rsqrt(ms + EPS) * gain * (1.0 - lambda_init)


def _split_maps(q, dk):
    lane = lax.broadcasted_iota(jnp.int32, q.shape, 1)
    zero = jnp.zeros_like(q)
    return jnp.concatenate([jnp.where(lane < dk, q, zero), jnp.where(lane >= dk, q, zero)], axis=0)


def _prompt_attn_body(lq_ref, sg_ref, q_ref, k_ref, v_ref, o_ref, *, tq, dk, lambda_init):
    qi = pl.program_id(2)
    qs = _split_maps(q_ref[...], dk)

    def step(j, carry, masked):
        m, l, acc = carry
        start = pl.multiple_of(j * tq, tq)
        k = k_ref[pl.ds(start, tq), :].astype(_BF16)
        v = v_ref[pl.ds(start, tq), :].astype(_BF16)
        s = lax.dot_general(qs, k, (((1,), (1,)), ((), ())), preferred_element_type=_F32)
        if masked:
            row = lax.broadcasted_iota(jnp.int32, s.shape, 0)
            row = jnp.where(row >= tq, row - tq, row)
            col = lax.broadcasted_iota(jnp.int32, s.shape, 1)
            s = jnp.where(col <= row, s, -jnp.inf)
        m_new = jnp.maximum(m, jnp.max(s, axis=1, keepdims=True))
        a = jnp.exp(m - m_new)
        p = jnp.exp(s - m_new)
        l = a * l + jnp.sum(p, axis=1, keepdims=True)
        acc = a * acc + jnp.dot(p.astype(_BF16), v, preferred_element_type=_F32)
        return m_new, l, acc

    init = (jnp.full((2 * tq, 1), -jnp.inf, _F32), jnp.zeros((2 * tq, 1), _F32),
            jnp.zeros((2 * tq, 2 * dk), _F32))
    carry = lax.fori_loop(0, qi, lambda j, c: step(j, c, False), init)
    _, l, acc = step(qi, carry, True)
    o = acc / l
    lam = _lambda_value(lq_ref[...], lambda_init)
    out = o[:tq] - lam * o[tq:]
    o_ref[...] = _sub_rms(out, sg_ref[...], lambda_init).astype(o_ref.dtype)


def _prompt_attention(q, k, proj, lq, subln, *, batch, seq, heads, dk, v_col0, lambda_init, tq, name):
    M = q.shape[0]
    dv = 2 * dk
    nq = seq // tq
    blocks = 2 * (_nbytes((tq, dv), _BF16) * 2 + 2 * _nbytes((seq, dv), _F32))
    scratch = 6 * _nbytes((2 * tq, tq), _F32)
    return pl.pallas_call(
        functools.partial(_prompt_attn_body, tq=tq, dk=dk, lambda_init=lambda_init),
        grid=(batch, heads, nq),
        in_specs=[pl.BlockSpec((4, dk), lambda b, h, i: (0, 0)),
                  pl.BlockSpec((1, dv), lambda b, h, i: (0, 0)),
                  pl.BlockSpec((tq, dv), lambda b, h, i: (b * nq + i, h)),
                  pl.BlockSpec((seq, dv), lambda b, h, i: (b, h)),
                  pl.BlockSpec((seq, dv), lambda b, h, i: (b, v_col0 // dv + h))],
        out_specs=pl.BlockSpec((tq, dv), lambda b, h, i: (b * nq + i, h)),
        out_shape=jax.ShapeDtypeStruct((M, heads * dv), _BF16),
        compiler_params=_compiler_params(("parallel", "parallel", "arbitrary"), blocks + scratch),
        name=name,
    )(lq, subln.reshape(1, dv), q, k, proj)


def _decode_attn_body(pt_ref, lq_ref, sg_ref, q_ref, kn_ref, vn_ref, *rest, n_fetch, heads, dk, lambda_init):
    k_refs = rest[:n_fetch]
    v_refs = rest[n_fetch:2 * n_fetch]
    o_ref = rest[2 * n_fetch]
    qm_ref, bias_ref, m_ref, l_ref, acc_ref = rest[2 * n_fetch + 1:]
    j = pl.program_id(1)

    @pl.when(j == 0)
    def _init():
        qm_ref[...] = _split_maps(q_ref[...], dk).astype(_BF16)
        row = lax.broadcasted_iota(jnp.int32, bias_ref.shape, 0)
        col = lax.broadcasted_iota(jnp.int32, bias_ref.shape, 1)
        bias_ref[...] = jnp.where((row & (heads - 1)) == (col & (heads - 1)), 0.0, -jnp.inf)
        m_ref[...] = jnp.full(m_ref.shape, -jnp.inf, _F32)
        l_ref[...] = jnp.zeros(l_ref.shape, _F32)
        acc_ref[...] = jnp.zeros(acc_ref.shape, _F32)

    qm = qm_ref[...]
    for k_ref, v_ref in zip(k_refs, v_refs):
        k = k_ref[...].astype(_BF16)
        v = v_ref[...].astype(_BF16)
        s = lax.dot_general(qm, k, (((1,), (1,)), ((), ())), preferred_element_type=_F32) + bias_ref[...]
        m = m_ref[...]
        m_new = jnp.maximum(m, jnp.max(s, axis=1, keepdims=True))
        a = jnp.exp(m - m_new)
        p = jnp.exp(s - m_new)
        l_ref[...] = a * l_ref[...] + jnp.sum(p, axis=1, keepdims=True)
        acc_ref[...] = a * acc_ref[...] + jnp.dot(p.astype(_BF16), v, preferred_element_type=_F32)
        m_ref[...] = m_new

    @pl.when(j == pl.num_programs(1) - 1)
    def _finish():
        prod = _split_maps(q_ref[...] * kn_ref[...], dk)
        s_self = jnp.sum(prod, axis=1, keepdims=True)
        v_self = jnp.concatenate([vn_ref[...], vn_ref[...]], axis=0)
        m = m_ref[...]
        m_new = jnp.maximum(m, s_self)
        a = jnp.exp(m - m_new)
        p = jnp.exp(s_self - m_new)
        l = a * l_ref[...] + p
        o = (a * acc_ref[...] + p * v_self) / l
        lam = _lambda_value(lq_ref[...], lambda_init)
        out = o[:heads] - lam * o[heads:]
        o_ref[...] = _sub_rms(out, sg_ref[...], lambda_init)


def _decode_attention(q, k_new, v_new, cache_k, cache_v, layer, page_table, lq, subln, *, dk, lambda_init,
                      n_fetch, name):
    B, heads, dv = q.shape
    depth, n_pool, page = cache_k.shape[:3]
    n_pages = page_table.shape[1]
    assert n_pages % n_fetch == 0 and heads & (heads - 1) == 0
    rows = page * heads
    ck = cache_k.reshape(depth, n_pool, rows, dv)
    cv = cache_v.reshape(depth, n_pool, rows, dv)

    def page_spec(r):
        return pl.BlockSpec((None, None, rows, dv),
                            lambda b, j, pt: (layer, pt[b * n_pages + j * n_fetch + r], 0, 0))

    head_spec = pl.BlockSpec((None, heads, dv), lambda b, j, pt: (b, 0, 0))
    blocks = 2 * (2 * n_fetch * _nbytes((rows, dv), _F32) + 4 * _nbytes((heads, dv), _F32))
    scratch = 4 * _nbytes((2 * heads, rows), _F32) + 2 * _nbytes((rows, dv), _BF16)
    grid_spec = pltpu.PrefetchScalarGridSpec(
        num_scalar_prefetch=1,
        grid=(B, n_pages // n_fetch),
        in_specs=[pl.BlockSpec((4, dk), lambda b, j, pt: (0, 0)),
                  pl.BlockSpec((1, dv), lambda b, j, pt: (0, 0)),
                  head_spec, head_spec, head_spec]
                 + [page_spec(r) for r in range(n_fetch)] + [page_spec(r) for r in range(n_fetch)],
        out_specs=head_spec,
        scratch_shapes=[pltpu.VMEM((2 * heads, dv), _BF16), pltpu.VMEM((2 * heads, rows), _F32),
                        pltpu.VMEM((2 * heads, 1), _F32), pltpu.VMEM((2 * heads, 1), _F32),
                        pltpu.VMEM((2 * heads, dv), _F32)],
    )
    return pl.pallas_call(
        functools.partial(_decode_attn_body, n_fetch=n_fetch, heads=heads, dk=dk, lambda_init=lambda_init),
        grid_spec=grid_spec,
        out_shape=jax.ShapeDtypeStruct((B, heads, dv), _F32),
        compiler_params=_compiler_params(("parallel", "arbitrary"), blocks + scratch),
        name=name,
    )(page_table.reshape(-1), lq, subln.reshape(1, dv), q, k_new, v_new,
      *([ck] * n_fetch), *([cv] * n_fetch))


def _hgrn_constants(c):
    t = np.arange(c)
    tt, uu = t[:, None], t[None, :]
    blocks = [(uu <= tt), (uu > tt)]
    masks = []
    s = c // 2
    while s >= 1:
        base = (tt // (2 * s)) * (2 * s)
        upper = (tt & s) != 0
        sel_upper = upper & (uu >= base + s) & (uu <= tt)
        sel_lower = (~upper) & (uu > tt) & (uu <= base + s - 1)
        blocks.append(sel_upper | sel_lower)
        x = tt ^ uu
        masks.append((x >= s) & (x < 2 * s) & (tt > uu))
        s //= 2
    sel = np.concatenate(blocks, axis=0).astype(np.float32)
    return jnp.asarray(sel, _BF16), jnp.asarray(np.stack(masks).astype(np.float32))


def _sigmoid(x):
    return 1.0 / (1.0 + jnp.exp(-x))


def _lower_bound(rows, layer):
    top = functools.reduce(jnp.maximum, rows)
    e = [jnp.exp(r - top) for r in rows]
    total = functools.reduce(lambda a, b: a + b, e)
    lb = jnp.zeros_like(total)
    for j in range(1, layer + 1):
        lb = lb + e[j] / total
    return lb


def _split3(x):
    hi = x.astype(_BF16)
    r1 = x - hi.astype(_F32)
    mid = r1.astype(_BF16)
    lo = (r1 - mid.astype(_F32)).astype(_BF16)
    return hi, mid, lo


def _hgrn_prompt_body(sel_ref, mask_ref, lbraw_ref, gn_ref, qh_ref, fh_ref, ih_ref, gh_ref, o_ref, s_ref,
                      st_ref, *, layer, chunk, n_chunks):
    n_levels = mask_ref.shape[0]
    width = qh_ref.shape[1]
    lb = _lower_bound([lbraw_ref[j:j + 1, :] for j in range(lbraw_ref.shape[0])], layer)
    gn = gn_ref[...]
    sel = sel_ref[...]
    st_ref[...] = jnp.zeros(st_ref.shape, _F32)

    def body(ci, _):
        rows = pl.ds(pl.multiple_of(ci * chunk, chunk), chunk)
        qh, fh, v, gh = qh_ref[rows, :], fh_ref[rows, :], ih_ref[rows, :], gh_ref[rows, :]
        fg = lb + (1.0 - lb) * _sigmoid(fh)
        q = qh * _sigmoid(qh)
        k = 1.0 - fg
        g = jnp.log(fg)
        hi, mid, lo = _split3(g)
        e3 = jnp.dot(sel, jnp.concatenate([hi, mid, lo], axis=1), preferred_element_type=_F32)
        dec = jnp.exp(e3[:, :width] + e3[:, width:2 * width] + e3[:, 2 * width:])
        v_b = v.astype(_BF16)
        st = st_ref[...]
        qd = (q * dec[:chunk]).astype(_BF16)
        o = lax.dot_general(qd, st.astype(_BF16), (((1,), (1,)), ((), ())), preferred_element_type=_F32)
        row = lax.broadcasted_iota(jnp.int32, (chunk, 1), 0)
        attn = jnp.zeros((chunk, chunk), _F32)
        for lvl in range(n_levels):
            s = chunk >> (lvl + 1)
            d = dec[(2 + lvl) * chunk:(3 + lvl) * chunk]
            upper = (row & s) != 0
            q_l = jnp.where(upper, q * d, 0.0).astype(_BF16)
            k_l = jnp.where(upper, 0.0, k * d).astype(_BF16)
            a_l = lax.dot_general(q_l, k_l, (((1,), (1,)), ((), ())), preferred_element_type=_F32)
            attn = attn + a_l * mask_ref[lvl]
        o = o + jnp.dot(attn.astype(_BF16), v_b, preferred_element_type=_F32)
        o = o + jnp.sum(q * k, axis=1, keepdims=True) * v
        k_dec = (k * dec[chunk:2 * chunk]).astype(_BF16)
        st_ref[...] = st * dec[chunk - 1:chunk] + lax.dot_general(
            v_b, k_dec, (((0,), (0,)), ((), ())), preferred_element_type=_F32)
        ms = jnp.mean(o * o, axis=-1, keepdims=True)
        o_ref[rows, :] = (o * lax.rsqrt(ms + EPS) * gn * (gh * _sigmoid(gh))).astype(o_ref.dtype)
        return 0

    lax.fori_loop(0, n_chunks, body, 0)
    s_ref[...] = st_ref[...].T


def _hgrn_prompt(proj, lb_raw, gnorm, layer, *, batch, seq, heads, col0, name):
    M = proj.shape[0]
    depth, n_lb = lb_raw.shape
    F = n_lb // heads
    I = gnorm.shape[0]
    assert F == LANES and I == LANES
    chunk = math.gcd(seq, HGRN_CHUNK)
    sel, masks = _hgrn_constants(chunk)

    def col(group):
        return pl.BlockSpec((seq, LANES), lambda b, h: (b, (col0 + group * heads * LANES) // LANES + h))

    blocks = 2 * (5 * _nbytes((seq, LANES), _F32) + _nbytes(sel.shape, _BF16) + _nbytes(masks.shape, _F32))
    return pl.pallas_call(
        functools.partial(_hgrn_prompt_body, layer=layer, chunk=chunk, n_chunks=seq // chunk),
        grid=(batch, heads),
        in_specs=[pl.BlockSpec(sel.shape, lambda b, h: (0, 0)),
                  pl.BlockSpec(masks.shape, lambda b, h: (0, 0, 0)),
                  pl.BlockSpec((depth, F), lambda b, h: (0, h)),
                  pl.BlockSpec((1, I), lambda b, h: (0, 0)),
                  col(0), col(1), col(2), col(3)],
        out_specs=[pl.BlockSpec((seq, I), lambda b, h: (b, h)),
                   pl.BlockSpec((None, None, F, I), lambda b, h: (b, h, 0, 0))],
        out_shape=[jax.ShapeDtypeStruct((M, heads * I), _BF16),
                   jax.ShapeDtypeStruct((batch, heads, F, I), _F32)],
        scratch_shapes=[pltpu.VMEM((I, F), _F32)],
        compiler_params=_compiler_params(("parallel", "parallel"), blocks + (4 << 20)),
        name=name,
    )(sel, masks, lb_raw, gnorm.reshape(1, I), proj, proj, proj, proj)


def _hgrn_step_body(lbraw_ref, gn_ref, qh_ref, fh_ref, ih_ref, gh_ref, s0_ref, o_ref, s_ref, *, layer, heads):
    F = fh_ref.shape[1]
    lb = _lower_bound([lbraw_ref[j] for j in range(lbraw_ref.shape[0])], layer)
    fg = lb + (1.0 - lb) * _sigmoid(fh_ref[...])
    qh = qh_ref[...]
    q = qh * _sigmoid(qh)
    k = 1.0 - fg
    pad = jnp.zeros((F - 3 * heads, F), _F32)
    cols = jnp.concatenate([fg, k, q, pad], axis=0).T
    v = ih_ref[...]
    outs = []
    for h in range(heads):
        s_new = cols[:, h:h + 1] * s0_ref[h] + cols[:, heads + h:heads + h + 1] * v[h:h + 1, :]
        s_ref[h] = s_new
        outs.append(jnp.sum(cols[:, 2 * heads + h:2 * heads + h + 1] * s_new, axis=0, keepdims=True))
    o = jnp.concatenate(outs, axis=0)
    gh = gh_ref[...]
    ms = jnp.mean(o * o, axis=-1, keepdims=True)
    o_ref[...] = o * lax.rsqrt(ms + EPS) * gn_ref[...] * (gh * _sigmoid(gh))


def _hgrn_step(proj3, lb_raw, gnorm, state, layer, *, heads, group0, name):
    B = proj3.shape[0]
    depth, n_lb = lb_raw.shape
    F, I = state.shape[3], state.shape[4]
    assert F == LANES and I == LANES and 3 * heads <= F

    def grp(g):
        return pl.BlockSpec((None, heads, LANES), lambda b: (b, group0 + g, 0))

    blocks = 2 * (2 * _nbytes((heads, F, I), _F32) + 6 * _nbytes((heads, LANES), _F32))
    return pl.pallas_call(
        functools.partial(_hgrn_step_body, layer=layer, heads=heads),
        grid=(B,),
        in_specs=[pl.BlockSpec((depth, heads, F), lambda b: (0, 0, 0)),
                  pl.BlockSpec((1, I), lambda b: (0, 0)),
                  grp(0), grp(1), grp(2), grp(3),
                  pl.BlockSpec((None, None, heads, F, I), lambda b: (layer, b, 0, 0, 0))],
        out_specs=[pl.BlockSpec((None, heads, I), lambda b: (b, 0, 0)),
                   pl.BlockSpec((None, heads, F, I), lambda b: (b, 0, 0, 0))],
        out_shape=[jax.ShapeDtypeStruct((B, heads, I), _F32),
                   jax.ShapeDtypeStruct((B, heads, F, I), _F32)],
        compiler_params=_compiler_params(("parallel",), blocks + (2 << 20)),
        name=name,
    )(lb_raw.reshape(depth, heads, F), gnorm.reshape(1, I), proj3, proj3, proj3, proj3, state)


def _pick(n, candidates):
    for c in candidates:
        if n % c == 0:
            return c
    return n


def _tiles(M, d_model, d_ff, in_cols):
    tm = _pick(M, (1024, 512, 256, 128, 64, 32, 16))
    return dict(
        tm=tm,
        tn_up=_pick(d_ff, (256, 128)),
        tm_down=min(tm, 512),
        tn_down=_pick(d_model, (512, 256, 128)),
        tn_in=_pick(in_cols, (1024, 512, 256, 128)),
        tn_out=_pick(d_model, (1024, 512, 256, 128)),
        tr=_pick(M, (256, 128, 64, 32, 16)),
    )


def _ffn_block(x, xb, wg, wu, wd, gain, bias, layer, alpha, t, tag):
    h = _ffn_up(xb, wg, wu, layer, tm=t["tm"], tn=t["tn_up"], name=f"ffn_up_{tag}")
    a = _matmul(h, wd, layer, tm=t["tm_down"], tn=t["tn_down"], name=f"ffn_down_{tag}")
    return _residual_ln(x, a, gain, bias, alpha=alpha, scale=0.5, tr=t["tr"], name=f"ln_{tag}")


def kernel(x_prompt, x_sample, cache_k, cache_v, state_hgrn, page_table, ln_gain, ln_bias, ffn1_gate, ffn1_up, ffn1_down, w_in, lambda_qk, subln_gain, hgrn_lower_bound, hgrn_gnorm_gain, w_out, ffn2_gate, ffn2_up, ffn2_down):
    batch, seq, d_model = x_prompt.shape
    dec_batch, dec_seq, _ = x_sample.shape
    assert dec_seq == 1
    depth = w_in.shape[0]
    att_heads, dv = cache_v.shape[3], cache_v.shape[4]
    dk = dv // 2
    assert cache_k.shape[4] == dv == LANES
    rot = dk // 4
    hg_heads, hg_f, hg_i = state_hgrn.shape[2:]
    d_ff = ffn1_gate.shape[2]
    in_cols = w_in.shape[2]
    att_width = att_heads * dv
    hg_col0 = 3 * att_width
    alpha = (2 * depth) ** 0.25
    att_scale = dk ** -0.5
    n_past = page_table.shape[1] * cache_k.shape[2]

    weights = [w.astype(_BF16) for w in (ffn1_gate, ffn1_up, ffn1_down, w_in, w_out, ffn2_gate, ffn2_up, ffn2_down)]
    f1g, f1u, f1d, win, wout, f2g, f2u, f2d = weights

    m_p = batch * seq
    m_s = 16
    t_p = _tiles(m_p, d_model, d_ff, in_cols)
    t_s = _tiles(m_s, d_model, d_ff, in_cols)
    tab_p = _rotary_tables(jnp.arange(seq, dtype=jnp.int32), dk, rot)
    tab_s = _rotary_tables(jnp.full((m_s,), n_past, jnp.int32), dk, rot)
    tq = _pick(seq, (256, 128))

    xp = x_prompt.reshape(m_p, d_model)
    xs = jnp.pad(x_sample.reshape(dec_batch, d_model), ((0, m_s - dec_batch), (0, 0)))
    xpb, xsb = xp.astype(_BF16), xs.astype(_BF16)

    outs = {k: [] for k in ("kp", "vp", "sp", "ks", "vs", "ss")}
    for l in range(depth):
        lambda_init = 0.8 - 0.6 * math.exp(-0.3 * l)
        gains, biases = ln_gain[l], ln_bias[l]

        xp, xpb = _ffn_block(xp, xpb, f1g, f1u, f1d, gains[0], biases[0], l, alpha, t_p, f"p{l}a")
        xs, xsb = _ffn_block(xs, xsb, f1g, f1u, f1d, gains[0], biases[0], l, alpha, t_s, f"s{l}a")

        proj = _matmul(xpb, win, l, tm=t_p["tm"], tn=t_p["tn_in"], name=f"w_in_p{l}")
        q_rot, k_rot = _rotary(proj, tab_p, heads=att_heads, width=dv, half=rot // 2, q_scale=att_scale,
                               tr=min(t_p["tr"], seq), q_dtype=_BF16, name=f"rotary_p{l}")
        att = _prompt_attention(q_rot, k_rot, proj, lambda_qk[l], subln_gain[l], batch=batch, seq=seq,
                                heads=att_heads, dk=dk, v_col0=2 * att_width, lambda_init=lambda_init, tq=tq,
                                name=f"attn_p{l}")
        o_h, s_new = _hgrn_prompt(proj, hgrn_lower_bound, hgrn_gnorm_gain[l], l, batch=batch, seq=seq,
                                  heads=hg_heads, col0=hg_col0, name=f"hgrn_p{l}")
        merged = jnp.concatenate([att, o_h], axis=1)
        mixed = _matmul(merged, wout, l, tm=t_p["tm"], tn=t_p["tn_out"], name=f"w_out_p{l}")
        xp, xpb = _residual_ln(xp, mixed, gains[1], biases[1], alpha=alpha, scale=1.0, tr=t_p["tr"],
                               name=f"ln_p{l}b")
        outs["kp"].append(k_rot.reshape(batch, seq, att_heads, dv))
        outs["vp"].append(proj[:, 2 * att_width:3 * att_width].reshape(batch, seq, att_heads, dv))
        outs["sp"].append(s_new)

        proj_s = _matmul(xsb, win, l, tm=t_s["tm"], tn=t_s["tn_in"], name=f"w_in_s{l}")
        q_s, k_s = _rotary(proj_s, tab_s, heads=att_heads, width=dv, half=rot // 2, q_scale=att_scale,
                           tr=m_s, q_dtype=_F32, name=f"rotary_s{l}")
        proj3 = proj_s[:dec_batch].reshape(dec_batch, in_cols // LANES, LANES)
        q3 = q_s[:dec_batch].reshape(dec_batch, att_heads, dv)
        k3 = k_s[:dec_batch].reshape(dec_batch, att_heads, dv)
        v3 = proj_s[:dec_batch, 2 * att_width:3 * att_width].reshape(dec_batch, att_heads, dv)
        att_s = _decode_attention(q3, k3, v3, cache_k, cache_v, l, page_table, lambda_qk[l], subln_gain[l],
                                  dk=dk, lambda_init=lambda_init, n_fetch=_pick(page_table.shape[1], (4, 2, 1)),
                                  name=f"attn_s{l}")
        o_s, st_s = _hgrn_step(proj3, hgrn_lower_bound, hgrn_gnorm_gain[l], state_hgrn, l, heads=hg_heads,
                               group0=hg_col0 // (hg_heads * LANES), name=f"hgrn_s{l}")
        merged_s = jnp.concatenate([att_s.reshape(dec_batch, att_width), o_s.reshape(dec_batch, hg_heads * hg_i)],
                                   axis=1)
        merged_s = jnp.pad(merged_s, ((0, m_s - dec_batch), (0, 0))).astype(_BF16)
        mixed_s = _matmul(merged_s, wout, l, tm=t_s["tm"], tn=t_s["tn_out"], name=f"w_out_s{l}")
        xs, xsb = _residual_ln(xs, mixed_s, gains[1], biases[1], alpha=alpha, scale=1.0, tr=t_s["tr"],
                               name=f"ln_s{l}b")
        outs["ks"].append(k3.reshape(dec_batch, 1, att_heads, dv))
        outs["vs"].append(v3.reshape(dec_batch, 1, att_heads, dv))
        outs["ss"].append(st_s)

        xp, xpb = _ffn_block(xp, xpb, f2g, f2u, f2d, gains[2], biases[2], l, alpha, t_p, f"p{l}c")
        xs, xsb = _ffn_block(xs, xsb, f2g, f2u, f2d, gains[2], biases[2], l, alpha, t_s, f"s{l}c")

    return (xp.reshape(batch, seq, d_model), xs[:dec_batch].reshape(dec_batch, 1, d_model),
            jnp.stack(outs["kp"]), jnp.stack(outs["vp"]), jnp.stack(outs["sp"]),
            jnp.stack(outs["ks"]), jnp.stack(outs["vs"]), jnp.stack(outs["ss"]))
```

```python
import functools
import math

import numpy as np
import jax
import jax.numpy as jnp
from jax import lax
from jax.experimental import pallas as pl
from jax.experimental.pallas import tpu as pltpu

_F32 = jnp.float32
_BF16 = jnp.bfloat16

EPS = 1e-5
ROPE_THETA = 500000.0
HGRN_CHUNK = 64
V7X_VMEM_BYTES = 64 * 1024 * 1024
LANES = 128


def _compiler_params(semantics, block_bytes):
    limit = min(int(block_bytes * 1.25) + (8 << 20), V7X_VMEM_BYTES - (6 << 20))
    return pltpu.CompilerParams(dimension_semantics=semantics, vmem_limit_bytes=limit)


def _nbytes(shape, dtype):
    return int(np.prod(shape)) * jnp.dtype(dtype).itemsize


def _mm_body(x_ref, w_ref, o_ref):
    o_ref[...] = jnp.dot(x_ref[...], w_ref[...], preferred_element_type=_F32).astype(o_ref.dtype)


def _matmul(x, w, layer, *, tm, tn, out_dtype=_F32, name):
    M, K = x.shape
    N = w.shape[2]
    assert M % tm == 0 and N % tn == 0
    blocks = 2 * (_nbytes((tm, K), x.dtype) + _nbytes((K, tn), w.dtype) + _nbytes((tm, tn), out_dtype))
    return pl.pallas_call(
        _mm_body,
        grid=(M // tm, N // tn),
        in_specs=[pl.BlockSpec((tm, K), lambda i, j: (i, 0)),
                  pl.BlockSpec((None, K, tn), lambda i, j: (layer, 0, j))],
        out_specs=pl.BlockSpec((tm, tn), lambda i, j: (i, j)),
        out_shape=jax.ShapeDtypeStruct((M, N), out_dtype),
        compiler_params=_compiler_params(("parallel", "arbitrary"), blocks + _nbytes((tm, tn), _F32)),
        name=name,
    )(x, w)


def _mm2_body(xa_ref, xb_ref, wa_ref, wb_ref, o_ref):
    o_ref[...] = (jnp.dot(xa_ref[...], wa_ref[...], preferred_element_type=_F32)
                  + jnp.dot(xb_ref[...], wb_ref[...], preferred_element_type=_F32)).astype(o_ref.dtype)


def _matmul_concat(xa, xb, w, layer, *, tm, tn, name):
    M, ka = xa.shape
    kb = xb.shape[1]
    N = w.shape[2]
    assert ka == kb and w.shape[1] == ka + kb and M % tm == 0 and N % tn == 0
    blocks = 2 * (2 * _nbytes((tm, ka), xa.dtype) + 2 * _nbytes((ka, tn), w.dtype) + _nbytes((tm, tn), _F32))
    return pl.pallas_call(
        _mm2_body,
        grid=(M // tm, N // tn),
        in_specs=[pl.BlockSpec((tm, ka), lambda i, j: (i, 0)),
                  pl.BlockSpec((tm, kb), lambda i, j: (i, 0)),
                  pl.BlockSpec((None, ka, tn), lambda i, j: (layer, 0, j)),
                  pl.BlockSpec((None, kb, tn), lambda i, j: (layer, 1, j))],
        out_specs=pl.BlockSpec((tm, tn), lambda i, j: (i, j)),
        out_shape=jax.ShapeDtypeStruct((M, N), _F32),
        compiler_params=_compiler_params(("parallel", "arbitrary"), blocks + 2 * _nbytes((tm, tn), _F32)),
        name=name,
    )(xa, xb, w, w)


def _ffn_up_body(x_ref, wg_ref, wu_ref, o_ref):
    x = x_ref[...]
    g = jnp.dot(x, wg_ref[...], preferred_element_type=_F32)
    u = jnp.dot(x, wu_ref[...], preferred_element_type=_F32)
    o_ref[...] = (g / (1.0 + jnp.exp(-g)) * u).astype(o_ref.dtype)


def _ffn_up(x, wg, wu, layer, *, tm, tn, name):
    M, K = x.shape
    N = wg.shape[2]
    assert M % tm == 0 and N % tn == 0
    blocks = 2 * (_nbytes((tm, K), x.dtype) + 2 * _nbytes((K, tn), wg.dtype) + _nbytes((tm, tn), _BF16))
    return pl.pallas_call(
        _ffn_up_body,
        grid=(M // tm, N // tn),
        in_specs=[pl.BlockSpec((tm, K), lambda i, j: (i, 0)),
                  pl.BlockSpec((None, K, tn), lambda i, j: (layer, 0, j)),
                  pl.BlockSpec((None, K, tn), lambda i, j: (layer, 0, j))],
        out_specs=pl.BlockSpec((tm, tn), lambda i, j: (i, j)),
        out_shape=jax.ShapeDtypeStruct((M, N), _BF16),
        compiler_params=_compiler_params(("parallel", "arbitrary"), blocks + 3 * _nbytes((tm, tn), _F32)),
        name=name,
    )(x, wg, wu)


def _ln_body(x_ref, a_ref, g_ref, b_ref, y_ref, yb_ref, *, alpha, scale):
    z = alpha * x_ref[...] + scale * a_ref[...]
    mu = jnp.mean(z, axis=-1, keepdims=True)
    zc = z - mu
    var = jnp.mean(zc * zc, axis=-1, keepdims=True)
    y = zc * lax.rsqrt(var + EPS) * g_ref[...] + b_ref[...]
    y_ref[...] = y
    yb_ref[...] = y.astype(_BF16)


def _residual_ln(x, a, gain, bias, *, alpha, scale, tr, name):
    M, D = x.shape
    assert M % tr == 0
    blocks = 2 * (3 * _nbytes((tr, D), _F32) + _nbytes((tr, D), _BF16))
    row = pl.BlockSpec((tr, D), lambda i: (i, 0))
    vec = pl.BlockSpec((1, D), lambda i: (0, 0))
    return pl.pallas_call(
        functools.partial(_ln_body, alpha=alpha, scale=scale),
        grid=(M // tr,),
        in_specs=[row, row, vec, vec],
        out_specs=[row, row],
        out_shape=[jax.ShapeDtypeStruct((M, D), _F32), jax.ShapeDtypeStruct((M, D), _BF16)],
        compiler_params=_compiler_params(("parallel",), blocks + 2 * _nbytes((tr, D), _F32)),
        name=name,
    )(x, a, gain.reshape(1, D), bias.reshape(1, D))


def _rotary_tables(pos, dk, rot):
    half = rot // 2
    inv_freq = jnp.exp(-math.log(ROPE_THETA) * jnp.arange(half, dtype=_F32) / half)
    ang = pos.astype(_F32)[:, None] * inv_freq[None, :]
    cos, sin = jnp.cos(ang), jnp.sin(ang)
    n = pos.shape[0]
    ones = jnp.ones((n, dk - rot), _F32)
    zeros_h = jnp.zeros((n, half), _F32)
    zeros_r = jnp.zeros((n, dk - rot), _F32)
    c = jnp.concatenate([cos, cos, ones], axis=1)
    s_hi = jnp.concatenate([-sin, zeros_h, zeros_r], axis=1)
    s_lo = jnp.concatenate([zeros_h, sin, zeros_r], axis=1)
    return tuple(jnp.concatenate([t, t], axis=1) for t in (c, s_hi, s_lo))


def _rotary_body(q_ref, k_ref, c_ref, shi_ref, slo_ref, qo_ref, ko_ref, *, heads, width, half, q_scale):
    c, s_hi, s_lo = c_ref[...], shi_ref[...], slo_ref[...]
    for h in range(heads):
        cols = slice(h * width, (h + 1) * width)
        for src, dst, scale in ((q_ref, qo_ref, q_scale), (k_ref, ko_ref, None)):
            x = src[:, cols]
            y = x * c + pltpu.roll(x, width - half, 1) * s_hi + pltpu.roll(x, half, 1) * s_lo
            if scale is not None:
                y = y * scale
            dst[:, cols] = y.astype(dst.dtype)


def _rotary(proj, tables, *, heads, width, half, q_scale, tr, q_dtype, name):
    M = proj.shape[0]
    W = heads * width
    n_tab = tables[0].shape[0] // tr
    blocks = 2 * (2 * _nbytes((tr, W), _F32) + 3 * _nbytes((tr, width), _F32)
                  + _nbytes((tr, W), q_dtype) + _nbytes((tr, W), _F32))
    tab = pl.BlockSpec((tr, width), lambda i: (i % n_tab, 0))
    return pl.pallas_call(
        functools.partial(_rotary_body, heads=heads, width=width, half=half, q_scale=q_scale),
        grid=(M // tr,),
        in_specs=[pl.BlockSpec((tr, W), lambda i: (i, 0)), pl.BlockSpec((tr, W), lambda i: (i, 1)),
                  tab, tab, tab],
        out_specs=[pl.BlockSpec((tr, W), lambda i: (i, 0)), pl.BlockSpec((tr, W), lambda i: (i, 0))],
        out_shape=[jax.ShapeDtypeStruct((M, W), q_dtype), jax.ShapeDtypeStruct((M, W), _F32)],
        compiler_params=_compiler_params(("parallel",), blocks),
        name=name,
    )(proj, proj, *tables)


def _lambda_value(lq, lambda_init):
    s01 = jnp.sum(lq[0:1, :] * lq[1:2, :], axis=1, keepdims=True)
    s23 = jnp.sum(lq[2:3, :] * lq[3:4, :], axis=1, keepdims=True)
    return jnp.exp(s01) - jnp.exp(s23) + lambda_init


def _sub_rms(x, gain, lambda_init):
    ms = jnp.mean(x * x, axis=-1, keepdims=True)
    return x * lax.rsqrt(ms + EPS) * gain * (1.0 - lambda_init)


def _split_maps(q, dk):
    lane = lax.broadcasted_iota(jnp.int32, q.shape, 1)
    zero = jnp.zeros_like(q)
    return jnp.concatenate([jnp.where(lane < dk, q, zero), jnp.where(lane >= dk, q, zero)], axis=0)


def _prompt_attn_body(lq_ref, sg_ref, q_ref, k_ref, v_ref, o_ref, kb_ref, vt_ref, *, tq, dk, lambda_init):
    qi = pl.program_id(2)
    n_blocks = k_ref.shape[0] // tq

    @pl.when(qi == 0)
    def _stage():
        for blk in range(n_blocks):
            rows = slice(blk * tq, (blk + 1) * tq)
            kb_ref[rows, :] = k_ref[rows, :].astype(_BF16)
            vt_ref[:, rows] = v_ref[rows, :].T.astype(_BF16)

    qt = q_ref[...].astype(_F32).T
    sub = lax.broadcasted_iota(jnp.int32, qt.shape, 0)
    qs = jnp.concatenate([jnp.where(sub < dk, qt, 0.0), jnp.where(sub >= dk, qt, 0.0)], axis=1).astype(_BF16)

    def scores(j):
        k = kb_ref[pl.ds(pl.multiple_of(j * tq, tq), tq), :]
        return jnp.dot(k, qs, preferred_element_type=_F32)

    def weighted_values(j, p):
        vt = vt_ref[:, pl.ds(pl.multiple_of(j * tq, tq), tq)]
        return jnp.dot(vt, p, preferred_element_type=_F32)

    def softmax_update(m, l, s):
        parts = []
        for c in range(0, s.shape[1], LANES):
            strip = slice(c, c + LANES)
            m_new = jnp.maximum(m[:, strip], jnp.max(s[:, strip], axis=0, keepdims=True))
            a = jnp.exp(m[:, strip] - m_new)
            p = jnp.exp(s[:, strip] - m_new)
            parts.append((m_new, a * l[:, strip] + jnp.sum(p, axis=0, keepdims=True), a, p.astype(_BF16)))
        return tuple(jnp.concatenate(x, axis=1) for x in zip(*parts))

    def trip(j, carry):
        m, l, acc, a_prev, p_prev, s = carry
        pv = weighted_values(jnp.maximum(j - 1, 0), p_prev)
        s_next = scores(j + 1)
        m, l, a, p = softmax_update(m, l, s)
        return m, l, a_prev * acc + pv, a, p, s_next

    init = (jnp.full((1, 2 * tq), -jnp.inf, _F32), jnp.zeros((1, 2 * tq), _F32),
            jnp.zeros((2 * dk, 2 * tq), _F32), jnp.ones((1, 2 * tq), _F32),
            jnp.zeros((tq, 2 * tq), _BF16), scores(0))
    m, l, acc, a_prev, p_prev, s = lax.fori_loop(0, qi, trip, init)
    acc = a_prev * acc + weighted_values(jnp.maximum(qi - 1, 0), p_prev)
    key = lax.broadcasted_iota(jnp.int32, s.shape, 0)
    qry = lax.broadcasted_iota(jnp.int32, s.shape, 1)
    qry = jnp.where(qry >= tq, qry - tq, qry)
    m, l, a, p = softmax_update(m, l, jnp.where(key <= qry, s, -jnp.inf))
    acc = a * acc + weighted_values(qi, p)
    o = acc / l
    lam = _lambda_value(lq_ref[...], lambda_init)
    out = o[:, :tq] - lam * o[:, tq:]
    ms = jnp.mean(out * out, axis=0, keepdims=True)
    out = out * lax.rsqrt(ms + EPS) * sg_ref[...] * (1.0 - lambda_init)
    o_ref[...] = out.T.astype(o_ref.dtype)


def _prompt_attention(q, k, proj, lq, subln, *, batch, seq, heads, dk, v_col0, lambda_init, tq, name):
    M = q.shape[0]
    dv = 2 * dk
    nq = seq // tq
    blocks = 2 * (_nbytes((tq, dv), _BF16) * 2 + 2 * _nbytes((seq, dv), _F32))
    scratch = 6 * _nbytes((2 * tq, tq), _F32)
    return pl.pallas_call(
        functools.partial(_prompt_attn_body, tq=tq, dk=dk, lambda_init=lambda_init),
        grid=(batch, heads, nq),
        in_specs=[pl.BlockSpec((4, dk), lambda b, h, i: (0, 0)),
                  pl.BlockSpec((dv, 1), lambda b, h, i: (0, 0)),
                  pl.BlockSpec((tq, dv), lambda b, h, i: (b * nq + i, h)),
                  pl.BlockSpec((seq, dv), lambda b, h, i: (b, h)),
                  pl.BlockSpec((seq, dv), lambda b, h, i: (b, v_col0 // dv + h))],
        out_specs=pl.BlockSpec((tq, dv), lambda b, h, i: (b * nq + i, h)),
        out_shape=jax.ShapeDtypeStruct((M, heads * dv), _BF16),
        scratch_shapes=[pltpu.VMEM((seq, dv), _BF16), pltpu.VMEM((dv, seq), _BF16)],
        compiler_params=_compiler_params(("parallel", "parallel", "arbitrary"), blocks + scratch),
        name=name,
    )(lq, subln.reshape(dv, 1), q, k, proj)


def _decode_attn_body(pt_ref, lq_ref, sg_ref, q_ref, kn_ref, vn_ref, *rest, n_fetch, heads, dk, lambda_init):
    k_refs = rest[:n_fetch]
    v_refs = rest[n_fetch:2 * n_fetch]
    o_ref = rest[2 * n_fetch]
    qm_ref, bias_ref, m_ref, l_ref, acc_ref = rest[2 * n_fetch + 1:]
    j = pl.program_id(1)

    @pl.when(j == 0)
    def _init():
        qm_ref[...] = _split_maps(q_ref[...], dk).astype(_BF16)
        row = lax.broadcasted_iota(jnp.int32, bias_ref.shape, 0)
        col = lax.broadcasted_iota(jnp.int32, bias_ref.shape, 1)
        bias_ref[...] = jnp.where((row & (heads - 1)) == (col & (heads - 1)), 0.0, -jnp.inf)
        m_ref[...] = jnp.full(m_ref.shape, -jnp.inf, _F32)
        l_ref[...] = jnp.zeros(l_ref.shape, _F32)
        acc_ref[...] = jnp.zeros(acc_ref.shape, _F32)

    qm = qm_ref[...]
    for k_ref, v_ref in zip(k_refs, v_refs):
        k = k_ref[...].astype(_BF16)
        v = v_ref[...].astype(_BF16)
        s = lax.dot_general(qm, k, (((1,), (1,)), ((), ())), preferred_element_type=_F32) + bias_ref[...]
        m = m_ref[...]
        m_new = jnp.maximum(m, jnp.max(s, axis=1, keepdims=True))
        a = jnp.exp(m - m_new)
        p = jnp.exp(s - m_new)
        l_ref[...] = a * l_ref[...] + jnp.sum(p, axis=1, keepdims=True)
        acc_ref[...] = a * acc_ref[...] + jnp.dot(p.astype(_BF16), v, preferred_element_type=_F32)
        m_ref[...] = m_new

    @pl.when(j == pl.num_programs(1) - 1)
    def _finish():
        prod = _split_maps(q_ref[...] * kn_ref[...], dk)
        s_self = jnp.sum(prod, axis=1, keepdims=True)
        v_self = jnp.concatenate([vn_ref[...], vn_ref[...]], axis=0)
        m = m_ref[...]
        m_new = jnp.maximum(m, s_self)
        a = jnp.exp(m - m_new)
        p = jnp.exp(s_self - m_new)
        l = a * l_ref[...] + p
        o = (a * acc_ref[...] + p * v_self) / l
        lam = _lambda_value(lq_ref[...], lambda_init)
        out = o[:heads] - lam * o[heads:]
        o_ref[...] = _sub_rms(out, sg_ref[...], lambda_init)


def _decode_attention(q, k_new, v_new, cache_k, cache_v, layer, page_table, lq, subln, *, dk, lambda_init,
                      n_fetch, name):
    B, heads, dv = q.shape
    depth, n_pool, page = cache_k.shape[:3]
    n_pages = page_table.shape[1]
    assert n_pages % n_fetch == 0 and heads & (heads - 1) == 0
    rows = page * heads
    ck = cache_k.reshape(depth, n_pool, rows, dv)
    cv = cache_v.reshape(depth, n_pool, rows, dv)

    def page_spec(r):
        return pl.BlockSpec((None, None, rows, dv),
                            lambda b, j, pt: (layer, pt[b * n_pages + j * n_fetch + r], 0, 0))

    head_spec = pl.BlockSpec((None, heads, dv), lambda b, j, pt: (b, 0, 0))
    blocks = 2 * (2 * n_fetch * _nbytes((rows, dv), _F32) + 4 * _nbytes((heads, dv), _F32))
    scratch = 4 * _nbytes((2 * heads, rows), _F32) + 2 * _nbytes((rows, dv), _BF16)
    grid_spec = pltpu.PrefetchScalarGridSpec(
        num_scalar_prefetch=1,
        grid=(B, n_pages // n_fetch),
        in_specs=[pl.BlockSpec((4, dk), lambda b, j, pt: (0, 0)),
                  pl.BlockSpec((1, dv), lambda b, j, pt: (0, 0)),
                  head_spec, head_spec, head_spec]
                 + [page_spec(r) for r in range(n_fetch)] + [page_spec(r) for r in range(n_fetch)],
        out_specs=head_spec,
        scratch_shapes=[pltpu.VMEM((2 * heads, dv), _BF16), pltpu.VMEM((2 * heads, rows), _F32),
                        pltpu.VMEM((2 * heads, 1), _F32), pltpu.VMEM((2 * heads, 1), _F32),
                        pltpu.VMEM((2 * heads, dv), _F32)],
    )
    return pl.pallas_call(
        functools.partial(_decode_attn_body, n_fetch=n_fetch, heads=heads, dk=dk, lambda_init=lambda_init),
        grid_spec=grid_spec,
        out_shape=jax.ShapeDtypeStruct((B, heads, dv), _F32),
        compiler_params=_compiler_params(("parallel", "arbitrary"), blocks + scratch),
        name=name,
    )(page_table.reshape(-1), lq, subln.reshape(1, dv), q, k_new, v_new,
      *([ck] * n_fetch), *([cv] * n_fetch))


def _hgrn_constants(c):
    t = np.arange(c)
    tt, uu = t[:, None], t[None, :]
    masks = []
    s = c // 2
    while s >= 1:
        x = tt ^ uu
        wide = np.zeros((c, 2 * c), np.float32)
        off = (len(masks) % 2) * c
        wide[:, off:off + c] = (x >= s) & (x < 2 * s) & (tt > uu)
        masks.append(wide)
        s //= 2
    return jnp.asarray(np.stack(masks))


def _level_reference(b, s):
    n = b.shape[0]
    if 2 * s >= 8:
        pieces = [jnp.broadcast_to(b[base + s - 1:base + s, :], (2 * s, b.shape[1])) for base in range(0, n, 2 * s)]
        return jnp.concatenate(pieces, axis=0)
    pos = lax.broadcasted_iota(jnp.int32, b.shape, 0) & (2 * s - 1)
    ref = b
    for p in range(2 * s):
        shift = p - (s - 1)
        if shift != 0:
            ref = jnp.where(pos == p, pltpu.roll(b, shift % n, 0), ref)
    return ref


def _sigmoid(x):
    return 1.0 / (1.0 + jnp.exp(-x))


def _lower_bound(rows, layer):
    top = functools.reduce(jnp.maximum, rows)
    e = [jnp.exp(r - top) for r in rows]
    total = functools.reduce(lambda a, b: a + b, e)
    lb = jnp.zeros_like(total)
    for j in range(1, layer + 1):
        lb = lb + e[j] / total
    return lb


def _hgrn_prompt_body(mask_ref, lbraw_ref, gn_ref, qh_ref, fh_ref, ih_ref, gh_ref, o_ref, s_ref, st_ref, *,
                      layer, chunk, n_chunks):
    n_levels = mask_ref.shape[0]
    width = LANES
    n_heads = st_ref.shape[0]
    lb_all = _lower_bound([lbraw_ref[j:j + 1, :] for j in range(lbraw_ref.shape[0])], layer)
    gn = gn_ref[...]
    row = lax.broadcasted_iota(jnp.int32, (chunk, width), 0)
    st_ref[...] = jnp.zeros(st_ref.shape, _F32)

    def head_chunk(ci, hh):
        rows = pl.ds(pl.multiple_of(ci * chunk, chunk), chunk)
        cols = slice(hh * width, (hh + 1) * width)
        lb = lb_all[:, cols]
        qh, fh, v, gh = qh_ref[rows, cols], fh_ref[rows, cols], ih_ref[rows, cols], gh_ref[rows, cols]
        fg = lb + (1.0 - lb) * _sigmoid(fh)
        q = qh * _sigmoid(qh)
        k = 1.0 - fg
        b = jnp.log(fg)
        d = 1
        while d < chunk:
            b = b + jnp.where(row >= d, pltpu.roll(b, d, 0), 0.0)
            d *= 2
        b_last = b[chunk - 1:chunk]

        def level_operands(s):
            upper = (row & s) != 0
            gap = b - _level_reference(b, s)
            dec = jnp.exp(jnp.where(upper, gap, -gap))
            return jnp.where(upper, q * dec, 0.0), jnp.where(upper, 0.0, k * dec)

        attn = jnp.zeros((chunk, 2 * chunk), _F32)
        for lvl in range(0, n_levels, 2):
            qa, ka = level_operands(chunk >> (lvl + 1))
            qb, kb = level_operands(chunk >> (lvl + 2))
            r = lax.dot_general(jnp.concatenate([qa, qb], axis=0).astype(_BF16),
                                jnp.concatenate([ka, kb], axis=0).astype(_BF16),
                                (((1,), (1,)), ((), ())), preferred_element_type=_F32)
            attn = attn + r[:chunk] * mask_ref[lvl] + r[chunk:] * mask_ref[lvl + 1]
        vt2 = jnp.concatenate([v, v], axis=0).T.astype(_BF16)
        st = st_ref[hh]
        lhs = jnp.concatenate([(q * jnp.exp(b)).astype(_BF16), attn.astype(_BF16)], axis=1)
        rhs = jnp.concatenate([st.astype(_BF16), vt2], axis=1)
        o = lax.dot_general(lhs, rhs, (((1,), (1,)), ((), ())), preferred_element_type=_F32)
        o = o + jnp.sum(q * k, axis=1, keepdims=True) * v
        k_dec = (k * jnp.exp(b_last - b)).astype(_BF16)
        st_ref[hh] = st * jnp.exp(b_last) + jnp.dot(vt2[:, :chunk], k_dec, preferred_element_type=_F32)
        ms = jnp.mean(o * o, axis=-1, keepdims=True)
        o_ref[rows, cols] = (o * lax.rsqrt(ms + EPS) * gn * (gh * _sigmoid(gh))).astype(o_ref.dtype)

    def body(ci, _):
        for hh in range(n_heads):
            head_chunk(ci, hh)
        return 0

    lax.fori_loop(0, n_chunks, body, 0)
    for hh in range(n_heads):
        s_ref[hh] = st_ref[hh].T


def _hgrn_prompt(proj, lb_raw, gnorm, layer, *, batch, seq, heads, col0, heads_per_step, name):
    M = proj.shape[0]
    depth, n_lb = lb_raw.shape
    F = n_lb // heads
    I = gnorm.shape[0]
    hp = heads_per_step
    assert F == LANES and I == LANES and heads % hp == 0 and col0 % (hp * LANES) == 0
    chunk = math.gcd(seq, HGRN_CHUNK)
    masks = _hgrn_constants(chunk)
    assert masks.shape[0] % 2 == 0 and 2 * chunk == LANES
    wide = hp * LANES

    def col(group):
        return pl.BlockSpec((seq, wide), lambda b, h: (b, (col0 + group * heads * LANES) // wide + h))

    blocks = 2 * (4 * _nbytes((seq, wide), _F32) + _nbytes((seq, wide), _BF16)
                  + _nbytes(masks.shape, _F32) + _nbytes((hp, F, I), _F32))
    return pl.pallas_call(
        functools.partial(_hgrn_prompt_body, layer=layer, chunk=chunk, n_chunks=seq // chunk),
        grid=(batch, heads // hp),
        in_specs=[pl.BlockSpec(masks.shape, lambda b, h: (0, 0, 0)),
                  pl.BlockSpec((depth, wide), lambda b, h: (0, h)),
                  pl.BlockSpec((1, I), lambda b, h: (0, 0)),
                  col(0), col(1), col(2), col(3)],
        out_specs=[pl.BlockSpec((seq, wide), lambda b, h: (b, h)),
                   pl.BlockSpec((None, hp, F, I), lambda b, h: (b, h, 0, 0))],
        out_shape=[jax.ShapeDtypeStruct((M, heads * I), _BF16),
                   jax.ShapeDtypeStruct((batch, heads, F, I), _F32)],
        scratch_shapes=[pltpu.VMEM((hp, I, F), _F32)],
        compiler_params=_compiler_params(("parallel", "parallel"), blocks + (4 << 20)),
        name=name,
    )(masks, lb_raw, gnorm.reshape(1, I), proj, proj, proj, proj)


def _hgrn_step_body(lbraw_ref, gn_ref, qh_ref, fh_ref, ih_ref, gh_ref, s0_ref, o_ref, s_ref, *, layer, heads):
    F = fh_ref.shape[1]
    lb = _lower_bound([lbraw_ref[j] for j in range(lbraw_ref.shape[0])], layer)
    fg = lb + (1.0 - lb) * _sigmoid(fh_ref[...])
    qh = qh_ref[...]
    q = qh * _sigmoid(qh)
    k = 1.0 - fg
    pad = jnp.zeros((F - 3 * heads, F), _F32)
    cols = jnp.concatenate([fg, k, q, pad], axis=0).T
    v = ih_ref[...]
    outs = []
    for h in range(heads):
        s_new = cols[:, h:h + 1] * s0_ref[h] + cols[:, heads + h:heads + h + 1] * v[h:h + 1, :]
        s_ref[h] = s_new
        outs.append(jnp.sum(cols[:, 2 * heads + h:2 * heads + h + 1] * s_new, axis=0, keepdims=True))
    o = jnp.concatenate(outs, axis=0)
    gh = gh_ref[...]
    ms = jnp.mean(o * o, axis=-1, keepdims=True)
    o_ref[...] = o * lax.rsqrt(ms + EPS) * gn_ref[...] * (gh * _sigmoid(gh))


def _hgrn_step(proj3, lb_raw, gnorm, state, layer, *, heads, group0, name):
    B = proj3.shape[0]
    depth, n_lb = lb_raw.shape
    F, I = state.shape[3], state.shape[4]
    assert F == LANES and I == LANES and 3 * heads <= F

    def grp(g):
        return pl.BlockSpec((None, heads, LANES), lambda b: (b, group0 + g, 0))

    blocks = 2 * (2 * _nbytes((heads, F, I), _F32) + 6 * _nbytes((heads, LANES), _F32))
    return pl.pallas_call(
        functools.partial(_hgrn_step_body, layer=layer, heads=heads),
        grid=(B,),
        in_specs=[pl.BlockSpec((depth, heads, F), lambda b: (0, 0, 0)),
                  pl.BlockSpec((1, I), lambda b: (0, 0)),
                  grp(0), grp(1), grp(2), grp(3),
                  pl.BlockSpec((None, None, heads, F, I), lambda b: (layer, b, 0, 0, 0))],
        out_specs=[pl.BlockSpec((None, heads, I), lambda b: (b, 0, 0)),
                   pl.BlockSpec((None, heads, F, I), lambda b: (b, 0, 0, 0))],
        out_shape=[jax.ShapeDtypeStruct((B, heads, I), _F32),
                   jax.ShapeDtypeStruct((B, heads, F, I), _F32)],
        compiler_params=_compiler_params(("parallel",), blocks + (2 << 20)),
        name=name,
    )(lb_raw.reshape(depth, heads, F), gnorm.reshape(1, I), proj3, proj3, proj3, proj3, state)


def _pick(n, candidates):
    for c in candidates:
        if n % c == 0:
            return c
    return n


def _tiles(M, d_model, d_ff, in_cols):
    tm = _pick(M, (1024, 512, 256, 128, 64, 32, 16))
    return dict(
        tm=tm,
        tn_up=_pick(d_ff, (256, 128)),
        tm_down=min(tm, 512),
        tn_down=_pick(d_model, (512, 256, 128)),
        tn_in=_pick(in_cols, (1024, 512, 256, 128)),
        tn_out=_pick(d_model, (1024, 512, 256, 128)),
        tr=_pick(M, (256, 128, 64, 32, 16)),
    )


def _ffn_block(x, xb, wg, wu, wd, gain, bias, layer, alpha, t, tag):
    h = _ffn_up(xb, wg, wu, layer, tm=t["tm"], tn=t["tn_up"], name=f"ffn_up_{tag}")
    a = _matmul(h, wd, layer, tm=t["tm_down"], tn=t["tn_down"], name=f"ffn_down_{tag}")
    return _residual_ln(x, a, gain, bias, alpha=alpha, scale=0.5, tr=t["tr"], name=f"ln_{tag}")


def kernel(x_prompt, x_sample, cache_k, cache_v, state_hgrn, page_table, ln_gain, ln_bias, ffn1_gate, ffn1_up, ffn1_down, w_in, lambda_qk, subln_gain, hgrn_lower_bound, hgrn_gnorm_gain, w_out, ffn2_gate, ffn2_up, ffn2_down):
    batch, seq, d_model = x_prompt.shape
    dec_batch, dec_seq, _ = x_sample.shape
    assert dec_seq == 1
    depth = w_in.shape[0]
    att_heads, dv = cache_v.shape[3], cache_v.shape[4]
    dk = dv // 2
    assert cache_k.shape[4] == dv == LANES
    rot = dk // 4
    hg_heads, hg_f, hg_i = state_hgrn.shape[2:]
    d_ff = ffn1_gate.shape[2]
    in_cols = w_in.shape[2]
    att_width = att_heads * dv
    hg_col0 = 3 * att_width
    alpha = (2 * depth) ** 0.25
    att_scale = dk ** -0.5
    n_past = page_table.shape[1] * cache_k.shape[2]

    weights = [w.astype(_BF16) for w in (ffn1_gate, ffn1_up, ffn1_down, w_in, w_out, ffn2_gate, ffn2_up, ffn2_down)]
    f1g, f1u, f1d, win, wout, f2g, f2u, f2d = weights

    m_p = batch * seq
    m_s = 16
    t_p = _tiles(m_p, d_model, d_ff, in_cols)
    t_s = _tiles(m_s, d_model, d_ff, in_cols)
    tab_p = _rotary_tables(jnp.arange(seq, dtype=jnp.int32), dk, rot)
    tab_s = _rotary_tables(jnp.full((m_s,), n_past, jnp.int32), dk, rot)
    tq = _pick(seq, (256, 128))

    xp = x_prompt.reshape(m_p, d_model)
    xs = jnp.pad(x_sample.reshape(dec_batch, d_model), ((0, m_s - dec_batch), (0, 0)))
    xpb, xsb = xp.astype(_BF16), xs.astype(_BF16)

    outs = {k: [] for k in ("kp", "vp", "sp", "ks", "vs", "ss")}
    for l in range(depth):
        lambda_init = 0.8 - 0.6 * math.exp(-0.3 * l)
        gains, biases = ln_gain[l], ln_bias[l]

        xp, xpb = _ffn_block(xp, xpb, f1g, f1u, f1d, gains[0], biases[0], l, alpha, t_p, f"p{l}a")
        xs, xsb = _ffn_block(xs, xsb, f1g, f1u, f1d, gains[0], biases[0], l, alpha, t_s, f"s{l}a")

        proj = _matmul(xpb, win, l, tm=t_p["tm"], tn=t_p["tn_in"], name=f"w_in_p{l}")
        q_rot, k_rot = _rotary(proj, tab_p, heads=att_heads, width=dv, half=rot // 2, q_scale=att_scale,
                               tr=min(t_p["tr"], seq), q_dtype=_BF16, name=f"rotary_p{l}")
        att = _prompt_attention(q_rot, k_rot, proj, lambda_qk[l], subln_gain[l], batch=batch, seq=seq,
                                heads=att_heads, dk=dk, v_col0=2 * att_width, lambda_init=lambda_init, tq=tq,
                                name=f"attn_p{l}")
        o_h, s_new = _hgrn_prompt(proj, hgrn_lower_bound, hgrn_gnorm_gain[l], l, batch=batch, seq=seq,
                                  heads=hg_heads, col0=hg_col0, heads_per_step=_pick(hg_heads, (4, 2, 1)),
                                  name=f"hgrn_p{l}")
        mixed = _matmul_concat(att, o_h, wout, l, tm=t_p["tm"], tn=t_p["tn_out"], name=f"w_out_p{l}")
        xp, xpb = _residual_ln(xp, mixed, gains[1], biases[1], alpha=alpha, scale=1.0, tr=t_p["tr"],
                               name=f"ln_p{l}b")
        outs["kp"].append(k_rot.reshape(batch, seq, att_heads, dv))
        outs["vp"].append(proj[:, 2 * att_width:3 * att_width].reshape(batch, seq, att_heads, dv))
        outs["sp"].append(s_new)

        proj_s = _matmul(xsb, win, l, tm=t_s["tm"], tn=t_s["tn_in"], name=f"w_in_s{l}")
        q_s, k_s = _rotary(proj_s, tab_s, heads=att_heads, width=dv, half=rot // 2, q_scale=att_scale,
                           tr=m_s, q_dtype=_F32, name=f"rotary_s{l}")
        proj3 = proj_s[:dec_batch].reshape(dec_batch, in_cols // LANES, LANES)
        q3 = q_s[:dec_batch].reshape(dec_batch, att_heads, dv)
        k3 = k_s[:dec_batch].reshape(dec_batch, att_heads, dv)
        v3 = proj_s[:dec_batch, 2 * att_width:3 * att_width].reshape(dec_batch, att_heads, dv)
        att_s = _decode_attention(q3, k3, v3, cache_k, cache_v, l, page_table, lambda_qk[l], subln_gain[l],
                                  dk=dk, lambda_init=lambda_init, n_fetch=_pick(page_table.shape[1], (4, 2, 1)),
                                  name=f"attn_s{l}")
        o_s, st_s = _hgrn_step(proj3, hgrn_lower_bound, hgrn_gnorm_gain[l], state_hgrn, l, heads=hg_heads,
                               group0=hg_col0 // (hg_heads * LANES), name=f"hgrn_s{l}")
        merged_s = jnp.concatenate([att_s.reshape(dec_batch, att_width), o_s.reshape(dec_batch, hg_heads * hg_i)],
                                   axis=1)
        merged_s = jnp.pad(merged_s, ((0, m_s - dec_batch), (0, 0))).astype(_BF16)
        mixed_s = _matmul(merged_s, wout, l, tm=t_s["tm"], tn=t_s["tn_out"], name=f"w_out_s{l}")
        xs, xsb = _residual_ln(xs, mixed_s, gains[1], biases[1], alpha=alpha, scale=1.0, tr=t_s["tr"],
                               name=f"ln_s{l}b")
        outs["ks"].append(k3.reshape(dec_batch, 1, att_heads, dv))
        outs["vs"].append(v3.reshape(dec_batch, 1, att_heads, dv))
        outs["ss"].append(st_s)

        xp, xpb = _ffn_block(xp, xpb, f2g, f2u, f2d, gains[2], biases[2], l, alpha, t_p, f"p{l}c")
        xs, xsb = _ffn_block(xs, xsb, f2g, f2u, f2d, gains[2], biases[2], l, alpha, t_s, f"s{l}c")

    return (xp.reshape(batch, seq, d_model), xs[:dec_batch].reshape(dec_batch, 1, d_model),
            jnp.stack(outs["kp"]), jnp.stack(outs["vp"]), jnp.stack(outs["sp"]),
            jnp.stack(outs["ks"]), jnp.stack(outs["vs"]), jnp.stack(outs["ss"]))
```

```python
import functools
import math

import numpy as np
import jax
import jax.numpy as jnp
from jax import lax
from jax.experimental import pallas as pl
from jax.experimental.pallas import tpu as pltpu

_F32 = jnp.float32
_BF16 = jnp.bfloat16

EPS = 1e-5
ROPE_THETA = 500000.0
HGRN_CHUNK = 64
V7X_VMEM_BYTES = 64 * 1024 * 1024
LANES = 128


def _compiler_params(semantics, block_bytes):
    limit = min(int(block_bytes * 1.25) + (8 << 20), V7X_VMEM_BYTES - (6 << 20))
    return pltpu.CompilerParams(dimension_semantics=semantics, vmem_limit_bytes=limit)


def _nbytes(shape, dtype):
    return int(np.prod(shape)) * jnp.dtype(dtype).itemsize


def _mm_body(x_ref, w_ref, o_ref):
    o_ref[...] = jnp.dot(x_ref[...], w_ref[...], preferred_element_type=_F32).astype(o_ref.dtype)


def _matmul(x, w, *, tm, tn, out_dtype=_F32, name):
    M, K = x.shape
    N = w.shape[1]
    assert M % tm == 0 and N % tn == 0
    blocks = 2 * (_nbytes((tm, K), x.dtype) + _nbytes((K, tn), w.dtype) + _nbytes((tm, tn), out_dtype))
    return pl.pallas_call(
        _mm_body,
        grid=(M // tm, N // tn),
        in_specs=[pl.BlockSpec((tm, K), lambda i, j: (i, 0)),
                  pl.BlockSpec((K, tn), lambda i, j: (0, j))],
        out_specs=pl.BlockSpec((tm, tn), lambda i, j: (i, j)),
        out_shape=jax.ShapeDtypeStruct((M, N), out_dtype),
        compiler_params=_compiler_params(("parallel", "arbitrary"), blocks + _nbytes((tm, tn), _F32)),
        name=name,
    )(x, w)


def _mm_cast_body(x_ref, w_ref, o_ref, wb_ref):
    wb = w_ref[...].astype(_BF16)
    wb_ref[...] = wb
    o_ref[...] = jnp.dot(x_ref[...], wb, preferred_element_type=_F32)


def _matmul_cast(x, w, layer, *, tn, name):
    M, K = x.shape
    N = w.shape[2]
    assert N % tn == 0
    blocks = 2 * (_nbytes((M, K), x.dtype) + _nbytes((K, tn), _F32) + _nbytes((K, tn), _BF16)
                  + _nbytes((M, tn), _F32))
    return pl.pallas_call(
        _mm_cast_body,
        grid=(N // tn,),
        in_specs=[pl.BlockSpec((M, K), lambda j: (0, 0)),
                  pl.BlockSpec((None, K, tn), lambda j: (layer, 0, j))],
        out_specs=[pl.BlockSpec((M, tn), lambda j: (0, j)), pl.BlockSpec((K, tn), lambda j: (0, j))],
        out_shape=[jax.ShapeDtypeStruct((M, N), _F32), jax.ShapeDtypeStruct((K, N), _BF16)],
        compiler_params=_compiler_params(("parallel",), blocks + _nbytes((K, tn), _BF16)),
        name=name,
    )(x, w)


def _mm2_body(xa_ref, xb_ref, wa_ref, wb_ref, o_ref):
    o_ref[...] = (jnp.dot(xa_ref[...], wa_ref[...], preferred_element_type=_F32)
                  + jnp.dot(xb_ref[...], wb_ref[...], preferred_element_type=_F32)).astype(o_ref.dtype)


def _matmul_concat(xa, xb, w, *, tm, tn, name):
    M, ka = xa.shape
    kb = xb.shape[1]
    N = w.shape[1]
    assert ka == kb and w.shape[0] == ka + kb and M % tm == 0 and N % tn == 0
    blocks = 2 * (2 * _nbytes((tm, ka), xa.dtype) + 2 * _nbytes((ka, tn), w.dtype) + _nbytes((tm, tn), _F32))
    return pl.pallas_call(
        _mm2_body,
        grid=(M // tm, N // tn),
        in_specs=[pl.BlockSpec((tm, ka), lambda i, j: (i, 0)),
                  pl.BlockSpec((tm, kb), lambda i, j: (i, 0)),
                  pl.BlockSpec((ka, tn), lambda i, j: (0, j)),
                  pl.BlockSpec((kb, tn), lambda i, j: (1, j))],
        out_specs=pl.BlockSpec((tm, tn), lambda i, j: (i, j)),
        out_shape=jax.ShapeDtypeStruct((M, N), _F32),
        compiler_params=_compiler_params(("parallel", "arbitrary"), blocks + 2 * _nbytes((tm, tn), _F32)),
        name=name,
    )(xa, xb, w, w)


def _ffn_up_body(x_ref, wg_ref, wu_ref, o_ref):
    x = x_ref[...]
    g = jnp.dot(x, wg_ref[...], preferred_element_type=_F32)
    u = jnp.dot(x, wu_ref[...], preferred_element_type=_F32)
    o_ref[...] = (g / (1.0 + jnp.exp(-g)) * u).astype(o_ref.dtype)


def _ffn_up(x, wg, wu, *, tm, tn, name):
    M, K = x.shape
    N = wg.shape[1]
    assert M % tm == 0 and N % tn == 0
    blocks = 2 * (_nbytes((tm, K), x.dtype) + 2 * _nbytes((K, tn), wg.dtype) + _nbytes((tm, tn), _BF16))
    return pl.pallas_call(
        _ffn_up_body,
        grid=(M // tm, N // tn),
        in_specs=[pl.BlockSpec((tm, K), lambda i, j: (i, 0)),
                  pl.BlockSpec((K, tn), lambda i, j: (0, j)),
                  pl.BlockSpec((K, tn), lambda i, j: (0, j))],
        out_specs=pl.BlockSpec((tm, tn), lambda i, j: (i, j)),
        out_shape=jax.ShapeDtypeStruct((M, N), _BF16),
        compiler_params=_compiler_params(("parallel", "arbitrary"), blocks + 3 * _nbytes((tm, tn), _F32)),
        name=name,
    )(x, wg, wu)


def _ffn_up_cast_body(x_ref, wg_ref, wu_ref, o_ref, wgb_ref, wub_ref):
    x = x_ref[...]
    wg = wg_ref[...].astype(_BF16)
    wu = wu_ref[...].astype(_BF16)
    wgb_ref[...] = wg
    wub_ref[...] = wu
    g = jnp.dot(x, wg, preferred_element_type=_F32)
    u = jnp.dot(x, wu, preferred_element_type=_F32)
    o_ref[...] = (g / (1.0 + jnp.exp(-g)) * u).astype(o_ref.dtype)


def _ffn_up_cast(x, wg, wu, layer, *, tn, name):
    M, K = x.shape
    N = wg.shape[2]
    assert N % tn == 0
    blocks = 2 * (_nbytes((M, K), x.dtype) + 2 * _nbytes((K, tn), _F32) + 2 * _nbytes((K, tn), _BF16)
                  + _nbytes((M, tn), _BF16))
    w_f32 = pl.BlockSpec((None, K, tn), lambda j: (layer, 0, j))
    w_b16 = pl.BlockSpec((K, tn), lambda j: (0, j))
    return pl.pallas_call(
        _ffn_up_cast_body,
        grid=(N // tn,),
        in_specs=[pl.BlockSpec((M, K), lambda j: (0, 0)), w_f32, w_f32],
        out_specs=[pl.BlockSpec((M, tn), lambda j: (0, j)), w_b16, w_b16],
        out_shape=[jax.ShapeDtypeStruct((M, N), _BF16), jax.ShapeDtypeStruct((K, N), _BF16),
                   jax.ShapeDtypeStruct((K, N), _BF16)],
        compiler_params=_compiler_params(("parallel",), blocks + 2 * _nbytes((K, tn), _BF16)),
        name=name,
    )(x, wg, wu)


def _ln_body(x_ref, a_ref, g_ref, b_ref, y_ref, yb_ref, *, alpha, scale):
    z = alpha * x_ref[...] + scale * a_ref[...]
    mu = jnp.mean(z, axis=-1, keepdims=True)
    zc = z - mu
    var = jnp.mean(zc * zc, axis=-1, keepdims=True)
    y = zc * lax.rsqrt(var + EPS) * g_ref[...] + b_ref[...]
    y_ref[...] = y
    yb_ref[...] = y.astype(_BF16)


def _residual_ln(x, a, gain, bias, *, alpha, scale, tr, name):
    M, D = x.shape
    assert M % tr == 0
    blocks = 2 * (3 * _nbytes((tr, D), _F32) + _nbytes((tr, D), _BF16))
    row = pl.BlockSpec((tr, D), lambda i: (i, 0))
    vec = pl.BlockSpec((1, D), lambda i: (0, 0))
    return pl.pallas_call(
        functools.partial(_ln_body, alpha=alpha, scale=scale),
        grid=(M // tr,),
        in_specs=[row, row, vec, vec],
        out_specs=[row, row],
        out_shape=[jax.ShapeDtypeStruct((M, D), _F32), jax.ShapeDtypeStruct((M, D), _BF16)],
        compiler_params=_compiler_params(("parallel",), blocks + 2 * _nbytes((tr, D), _F32)),
        name=name,
    )(x, a, gain.reshape(1, D), bias.reshape(1, D))


def _rotary_tables(pos, dk, rot):
    half = rot // 2
    inv_freq = jnp.exp(-math.log(ROPE_THETA) * jnp.arange(half, dtype=_F32) / half)
    ang = pos.astype(_F32)[:, None] * inv_freq[None, :]
    cos, sin = jnp.cos(ang), jnp.sin(ang)
    n = pos.shape[0]
    ones = jnp.ones((n, dk - rot), _F32)
    zeros_h = jnp.zeros((n, half), _F32)
    zeros_r = jnp.zeros((n, dk - rot), _F32)
    c = jnp.concatenate([cos, cos, ones], axis=1)
    s_hi = jnp.concatenate([-sin, zeros_h, zeros_r], axis=1)
    s_lo = jnp.concatenate([zeros_h, sin, zeros_r], axis=1)
    return tuple(jnp.concatenate([t, t], axis=1) for t in (c, s_hi, s_lo))


def _rotary_body(q_ref, k_ref, c_ref, shi_ref, slo_ref, qo_ref, ko_ref, *, heads, width, half, q_scale):
    c, s_hi, s_lo = c_ref[...], shi_ref[...], slo_ref[...]
    for h in range(heads):
        cols = slice(h * width, (h + 1) * width)
        for src, dst, scale in ((q_ref, qo_ref, q_scale), (k_ref, ko_ref, None)):
            x = src[:, cols]
            y = x * c + pltpu.roll(x, width - half, 1) * s_hi + pltpu.roll(x, half, 1) * s_lo
            if scale is not None:
                y = y * scale
            dst[:, cols] = y.astype(dst.dtype)


def _rotary(proj, tables, *, heads, width, half, q_scale, tr, q_dtype, name):
    M = proj.shape[0]
    W = heads * width
    n_tab = tables[0].shape[0] // tr
    blocks = 2 * (2 * _nbytes((tr, W), _F32) + 3 * _nbytes((tr, width), _F32)
                  + _nbytes((tr, W), q_dtype) + _nbytes((tr, W), _F32))
    tab = pl.BlockSpec((tr, width), lambda i: (i % n_tab, 0))
    return pl.pallas_call(
        functools.partial(_rotary_body, heads=heads, width=width, half=half, q_scale=q_scale),
        grid=(M // tr,),
        in_specs=[pl.BlockSpec((tr, W), lambda i: (i, 0)), pl.BlockSpec((tr, W), lambda i: (i, 1)),
                  tab, tab, tab],
        out_specs=[pl.BlockSpec((tr, W), lambda i: (i, 0)), pl.BlockSpec((tr, W), lambda i: (i, 0))],
        out_shape=[jax.ShapeDtypeStruct((M, W), q_dtype), jax.ShapeDtypeStruct((M, W), _F32)],
        compiler_params=_compiler_params(("parallel",), blocks),
        name=name,
    )(proj, proj, *tables)


def _lambda_value(lq, lambda_init):
    s01 = jnp.sum(lq[0:1, :] * lq[1:2, :], axis=1, keepdims=True)
    s23 = jnp.sum(lq[2:3, :] * lq[3:4, :], axis=1, keepdims=True)
    return jnp.exp(s01) - jnp.exp(s23) + lambda_init


def _sub_rms(x, gain, lambda_init):
    ms = jnp.mean(x * x, axis=-1, keepdims=True)
    return x * lax.rsqrt(ms + EPS) * gain * (1.0 - lambda_init)


def _split_maps(q, dk):
    lane = lax.broadcasted_iota(jnp.int32, q.shape, 1)
    zero = jnp.zeros_like(q)
    return jnp.concatenate([jnp.where(lane < dk, q, zero), jnp.where(lane >= dk, q, zero)], axis=0)


def _prompt_attn_body(lq_ref, sg_ref, q_ref, k_ref, v_ref, o_ref, kb_ref, vt_ref, s_ref, p_ref, acc_ref, *,
                      tq, dk, lambda_init):
    qi = pl.program_id(2)
    n_blocks = k_ref.shape[0] // tq

    @pl.when(qi == 0)
    def _stage():
        for blk in range(n_blocks):
            rows = slice(blk * tq, (blk + 1) * tq)
            kb_ref[rows, :] = k_ref[rows, :].astype(_BF16)
            vt_ref[:, rows] = v_ref[rows, :].T.astype(_BF16)

    qt = q_ref[...].astype(_F32).T
    sub = lax.broadcasted_iota(jnp.int32, qt.shape, 0)
    qs = jnp.concatenate([jnp.where(sub < dk, qt, 0.0), jnp.where(sub >= dk, qt, 0.0)], axis=1).astype(_BF16)

    def scores(j):
        k = kb_ref[pl.ds(pl.multiple_of(j * tq, tq), tq), :]
        return jnp.dot(k, qs, preferred_element_type=_F32)

    def weighted_values(j, p):
        vt = vt_ref[:, pl.ds(pl.multiple_of(j * tq, tq), tq)]
        return jnp.dot(vt, p, preferred_element_type=_F32)

    def softmax_update(m, l, slot, masked):
        parts = []
        for c in range(0, 2 * tq, LANES):
            strip = slice(c, c + LANES)
            s = s_ref[slot, :, strip]
            if masked:
                key = lax.broadcasted_iota(jnp.int32, s.shape, 0)
                qry = lax.broadcasted_iota(jnp.int32, s.shape, 1) + (c % tq)
                s = jnp.where(key <= qry, s, -jnp.inf)
            m_new = jnp.maximum(m[:, strip], jnp.max(s, axis=0, keepdims=True))
            a = jnp.exp(m[:, strip] - m_new)
            p = jnp.exp(s - m_new)
            p_ref[slot, :, strip] = p.astype(_BF16)
            parts.append((m_new, a * l[:, strip] + jnp.sum(p, axis=0, keepdims=True), a))
        return tuple(jnp.concatenate(x, axis=1) for x in zip(*parts))

    def trip(j, carry):
        m, l, a_prev = carry
        cur = lax.rem(j, 2)
        pv = weighted_values(jnp.maximum(j - 1, 0), p_ref[1 - cur])
        m, l, a = softmax_update(m, l, cur, False)
        s_ref[1 - cur] = scores(j + 1)
        acc_ref[...] = a_prev * acc_ref[...] + pv
        return m, l, a

    s_ref[0] = scores(0)
    p_ref[1] = jnp.zeros(p_ref.shape[1:], _BF16)
    acc_ref[...] = jnp.zeros(acc_ref.shape, _F32)
    init = (jnp.full((1, 2 * tq), -jnp.inf, _F32), jnp.zeros((1, 2 * tq), _F32), jnp.ones((1, 2 * tq), _F32))
    m, l, a_prev = lax.fori_loop(0, qi, trip, init)
    cur = lax.rem(qi, 2)
    acc = a_prev * acc_ref[...] + weighted_values(jnp.maximum(qi - 1, 0), p_ref[1 - cur])
    m, l, a = softmax_update(m, l, cur, True)
    acc = a * acc + weighted_values(qi, p_ref[cur])
    o = acc / l
    lam = _lambda_value(lq_ref[...], lambda_init)
    out = o[:, :tq] - lam * o[:, tq:]
    ms = jnp.mean(out * out, axis=0, keepdims=True)
    out = out * lax.rsqrt(ms + EPS) * sg_ref[...] * (1.0 - lambda_init)
    o_ref[...] = out.T.astype(o_ref.dtype)


def _prompt_attention(q, k, proj, lq, subln, *, batch, seq, heads, dk, v_col0, lambda_init, tq, name):
    M = q.shape[0]
    dv = 2 * dk
    nq = seq // tq
    blocks = 2 * (_nbytes((tq, dv), _BF16) * 2 + 2 * _nbytes((seq, dv), _F32))
    scratch = 6 * _nbytes((2 * tq, tq), _F32)
    return pl.pallas_call(
        functools.partial(_prompt_attn_body, tq=tq, dk=dk, lambda_init=lambda_init),
        grid=(batch, heads, nq),
        in_specs=[pl.BlockSpec((4, dk), lambda b, h, i: (0, 0)),
                  pl.BlockSpec((dv, 1), lambda b, h, i: (0, 0)),
                  pl.BlockSpec((tq, dv), lambda b, h, i: (b * nq + i, h)),
                  pl.BlockSpec((seq, dv), lambda b, h, i: (b, h)),
                  pl.BlockSpec((seq, dv), lambda b, h, i: (b, v_col0 // dv + h))],
        out_specs=pl.BlockSpec((tq, dv), lambda b, h, i: (b * nq + i, h)),
        out_shape=jax.ShapeDtypeStruct((M, heads * dv), _BF16),
        scratch_shapes=[pltpu.VMEM((seq, dv), _BF16), pltpu.VMEM((dv, seq), _BF16),
                        pltpu.VMEM((2, tq, 2 * tq), _F32), pltpu.VMEM((2, tq, 2 * tq), _BF16),
                        pltpu.VMEM((dv, 2 * tq), _F32)],
        compiler_params=_compiler_params(("parallel", "parallel", "arbitrary"), blocks + scratch),
        name=name,
    )(lq, subln.reshape(dv, 1), q, k, proj)


def _decode_attn_body(pt_ref, lq_ref, sg_ref, q_ref, kn_ref, vn_ref, *rest, n_fetch, heads, dk, lambda_init):
    k_refs = rest[:n_fetch]
    v_refs = rest[n_fetch:2 * n_fetch]
    o_ref = rest[2 * n_fetch]
    qm_ref, bias_ref, m_ref, l_ref, acc_ref = rest[2 * n_fetch + 1:]
    j = pl.program_id(1)

    @pl.when(j == 0)
    def _init():
        qm_ref[...] = _split_maps(q_ref[...], dk).astype(_BF16)
        row = lax.broadcasted_iota(jnp.int32, bias_ref.shape, 0)
        col = lax.broadcasted_iota(jnp.int32, bias_ref.shape, 1)
        bias_ref[...] = jnp.where((row & (heads - 1)) == (col & (heads - 1)), 0.0, -jnp.inf)
        m_ref[...] = jnp.full(m_ref.shape, -jnp.inf, _F32)
        l_ref[...] = jnp.zeros(l_ref.shape, _F32)
        acc_ref[...] = jnp.zeros(acc_ref.shape, _F32)

    qm = qm_ref[...]
    for k_ref, v_ref in zip(k_refs, v_refs):
        k = k_ref[...].astype(_BF16)
        v = v_ref[...].astype(_BF16)
        s = lax.dot_general(qm, k, (((1,), (1,)), ((), ())), preferred_element_type=_F32) + bias_ref[...]
        m = m_ref[...]
        m_new = jnp.maximum(m, jnp.max(s, axis=1, keepdims=True))
        a = jnp.exp(m - m_new)
        p = jnp.exp(s - m_new)
        l_ref[...] = a * l_ref[...] + jnp.sum(p, axis=1, keepdims=True)
        acc_ref[...] = a * acc_ref[...] + jnp.dot(p.astype(_BF16), v, preferred_element_type=_F32)
        m_ref[...] = m_new

    @pl.when(j == pl.num_programs(1) - 1)
    def _finish():
        prod = _split_maps(q_ref[...] * kn_ref[...], dk)
        s_self = jnp.sum(prod, axis=1, keepdims=True)
        v_self = jnp.concatenate([vn_ref[...], vn_ref[...]], axis=0)
        m = m_ref[...]
        m_new = jnp.maximum(m, s_self)
        a = jnp.exp(m - m_new)
        p = jnp.exp(s_self - m_new)
        l = a * l_ref[...] + p
        o = (a * acc_ref[...] + p * v_self) / l
        lam = _lambda_value(lq_ref[...], lambda_init)
        out = o[:heads] - lam * o[heads:]
        o_ref[...] = _sub_rms(out, sg_ref[...], lambda_init)


def _decode_attention(q, k_new, v_new, cache_k, cache_v, layer, page_table, lq, subln, *, dk, lambda_init,
                      n_fetch, name):
    B, heads, dv = q.shape
    depth, n_pool, page = cache_k.shape[:3]
    n_pages = page_table.shape[1]
    assert n_pages % n_fetch == 0 and heads & (heads - 1) == 0
    rows = page * heads
    ck = cache_k.reshape(depth, n_pool, rows, dv)
    cv = cache_v.reshape(depth, n_pool, rows, dv)

    def page_spec(r):
        return pl.BlockSpec((None, None, rows, dv),
                            lambda b, j, pt: (layer, pt[b * n_pages + j * n_fetch + r], 0, 0))

    head_spec = pl.BlockSpec((None, heads, dv), lambda b, j, pt: (b, 0, 0))
    blocks = 2 * (2 * n_fetch * _nbytes((rows, dv), _F32) + 4 * _nbytes((heads, dv), _F32))
    scratch = 4 * _nbytes((2 * heads, rows), _F32) + 2 * _nbytes((rows, dv), _BF16)
    grid_spec = pltpu.PrefetchScalarGridSpec(
        num_scalar_prefetch=1,
        grid=(B, n_pages // n_fetch),
        in_specs=[pl.BlockSpec((4, dk), lambda b, j, pt: (0, 0)),
                  pl.BlockSpec((1, dv), lambda b, j, pt: (0, 0)),
                  head_spec, head_spec, head_spec]
                 + [page_spec(r) for r in range(n_fetch)] + [page_spec(r) for r in range(n_fetch)],
        out_specs=head_spec,
        scratch_shapes=[pltpu.VMEM((2 * heads, dv), _BF16), pltpu.VMEM((2 * heads, rows), _F32),
                        pltpu.VMEM((2 * heads, 1), _F32), pltpu.VMEM((2 * heads, 1), _F32),
                        pltpu.VMEM((2 * heads, dv), _F32)],
    )
    return pl.pallas_call(
        functools.partial(_decode_attn_body, n_fetch=n_fetch, heads=heads, dk=dk, lambda_init=lambda_init),
        grid_spec=grid_spec,
        out_shape=jax.ShapeDtypeStruct((B, heads, dv), _F32),
        compiler_params=_compiler_params(("parallel", "arbitrary"), blocks + scratch),
        name=name,
    )(page_table.reshape(-1), lq, subln.reshape(1, dv), q, k_new, v_new,
      *([ck] * n_fetch), *([cv] * n_fetch))


def _hgrn_constants(c):
    t = np.arange(c)
    tt, uu = t[:, None], t[None, :]
    masks = []
    s = c // 2
    while s >= 1:
        x = tt ^ uu
        wide = np.zeros((c, 2 * c), np.float32)
        off = (len(masks) % 2) * c
        wide[:, off:off + c] = (x >= s) & (x < 2 * s) & (tt > uu)
        masks.append(wide)
        s //= 2
    return jnp.asarray(np.stack(masks))


def _level_reference(b, s):
    n = b.shape[0]
    if 2 * s >= 8:
        pieces = [jnp.broadcast_to(b[base + s - 1:base + s, :], (2 * s, b.shape[1])) for base in range(0, n, 2 * s)]
        return jnp.concatenate(pieces, axis=0)
    pos = lax.broadcasted_iota(jnp.int32, b.shape, 0) & (2 * s - 1)
    ref = b
    for p in range(2 * s):
        shift = p - (s - 1)
        if shift != 0:
            ref = jnp.where(pos == p, pltpu.roll(b, shift % n, 0), ref)
    return ref


def _sigmoid(x):
    return 1.0 / (1.0 + jnp.exp(-x))


def _lower_bound(rows, layer):
    top = functools.reduce(jnp.maximum, rows)
    e = [jnp.exp(r - top) for r in rows]
    total = functools.reduce(lambda a, b: a + b, e)
    lb = jnp.zeros_like(total)
    for j in range(1, layer + 1):
        lb = lb + e[j] / total
    return lb


def _hgrn_prompt_body(mask_ref, lbraw_ref, gn_ref, qh_ref, fh_ref, ih_ref, gh_ref, o_ref, s_ref, st_ref, *,
                      layer, chunk, n_chunks):
    n_levels = mask_ref.shape[0]
    width = LANES
    n_heads = st_ref.shape[0]
    lb_all = _lower_bound([lbraw_ref[j:j + 1, :] for j in range(lbraw_ref.shape[0])], layer)
    gn = gn_ref[...]
    row = lax.broadcasted_iota(jnp.int32, (chunk, width), 0)
    st_ref[...] = jnp.zeros(st_ref.shape, _F32)

    def head_chunk(ci, hh):
        rows = pl.ds(pl.multiple_of(ci * chunk, chunk), chunk)
        cols = slice(hh * width, (hh + 1) * width)
        lb = lb_all[:, cols]
        qh, fh, v, gh = qh_ref[rows, cols], fh_ref[rows, cols], ih_ref[rows, cols], gh_ref[rows, cols]
        fg = lb + (1.0 - lb) * _sigmoid(fh)
        q = qh * _sigmoid(qh)
        k = 1.0 - fg
        b = jnp.log(fg)
        d = 1
        while d < chunk:
            b = b + jnp.where(row >= d, pltpu.roll(b, d, 0), 0.0)
            d *= 2
        b_last = b[chunk - 1:chunk]

        def level_operands(s):
            upper = (row & s) != 0
            gap = b - _level_reference(b, s)
            dec = jnp.exp(jnp.where(upper, gap, -gap))
            return jnp.where(upper, q * dec, 0.0), jnp.where(upper, 0.0, k * dec)

        attn = jnp.zeros((chunk, 2 * chunk), _F32)
        for lvl in range(0, n_levels, 2):
            qa, ka = level_operands(chunk >> (lvl + 1))
            qb, kb = level_operands(chunk >> (lvl + 2))
            r = lax.dot_general(jnp.concatenate([qa, qb], axis=0).astype(_BF16),
                                jnp.concatenate([ka, kb], axis=0).astype(_BF16),
                                (((1,), (1,)), ((), ())), preferred_element_type=_F32)
            attn = attn + r[:chunk] * mask_ref[lvl] + r[chunk:] * mask_ref[lvl + 1]
        vt2 = jnp.concatenate([v, v], axis=0).T.astype(_BF16)
        st = st_ref[hh]
        lhs = jnp.concatenate([(q * jnp.exp(b)).astype(_BF16), attn.astype(_BF16)], axis=1)
        rhs = jnp.concatenate([st.astype(_BF16), vt2], axis=1)
        o = lax.dot_general(lhs, rhs, (((1,), (1,)), ((), ())), preferred_element_type=_F32)
        o = o + jnp.sum(q * k, axis=1, keepdims=True) * v
        k_dec = (k * jnp.exp(b_last - b)).astype(_BF16)
        st_ref[hh] = st * jnp.exp(b_last) + jnp.dot(vt2[:, :chunk], k_dec, preferred_element_type=_F32)
        ms = jnp.mean(o * o, axis=-1, keepdims=True)
        o_ref[rows, cols] = (o * lax.rsqrt(ms + EPS) * gn * (gh * _sigmoid(gh))).astype(o_ref.dtype)

    def body(ci, _):
        for hh in range(n_heads):
            head_chunk(ci, hh)
        return 0

    lax.fori_loop(0, n_chunks, body, 0)
    for hh in range(n_heads):
        s_ref[hh] = st_ref[hh].T


def _hgrn_prompt(proj, lb_raw, gnorm, layer, *, batch, seq, heads, col0, heads_per_step, name):
    M = proj.shape[0]
    depth, n_lb = lb_raw.shape
    F = n_lb // heads
    I = gnorm.shape[0]
    hp = heads_per_step
    assert F == LANES and I == LANES and heads % hp == 0 and col0 % (hp * LANES) == 0
    chunk = math.gcd(seq, HGRN_CHUNK)
    masks = _hgrn_constants(chunk)
    assert masks.shape[0] % 2 == 0 and 2 * chunk == LANES
    wide = hp * LANES

    def col(group):
        return pl.BlockSpec((seq, wide), lambda b, h: (b, (col0 + group * heads * LANES) // wide + h))

    blocks = 2 * (4 * _nbytes((seq, wide), _F32) + _nbytes((seq, wide), _BF16)
                  + _nbytes(masks.shape, _F32) + _nbytes((hp, F, I), _F32))
    return pl.pallas_call(
        functools.partial(_hgrn_prompt_body, layer=layer, chunk=chunk, n_chunks=seq // chunk),
        grid=(batch, heads // hp),
        in_specs=[pl.BlockSpec(masks.shape, lambda b, h: (0, 0, 0)),
                  pl.BlockSpec((depth, wide), lambda b, h: (0, h)),
                  pl.BlockSpec((1, I), lambda b, h: (0, 0)),
                  col(0), col(1), col(2), col(3)],
        out_specs=[pl.BlockSpec((seq, wide), lambda b, h: (b, h)),
                   pl.BlockSpec((None, hp, F, I), lambda b, h: (b, h, 0, 0))],
        out_shape=[jax.ShapeDtypeStruct((M, heads * I), _BF16),
                   jax.ShapeDtypeStruct((batch, heads, F, I), _F32)],
        scratch_shapes=[pltpu.VMEM((hp, I, F), _F32)],
        compiler_params=_compiler_params(("parallel", "parallel"), blocks + (4 << 20)),
        name=name,
    )(masks, lb_raw, gnorm.reshape(1, I), proj, proj, proj, proj)


def _hgrn_step_body(lbraw_ref, gn_ref, qh_ref, fh_ref, ih_ref, gh_ref, s0_ref, o_ref, s_ref, *, layer, heads):
    F = fh_ref.shape[1]
    lb = _lower_bound([lbraw_ref[j] for j in range(lbraw_ref.shape[0])], layer)
    fg = lb + (1.0 - lb) * _sigmoid(fh_ref[...])
    qh = qh_ref[...]
    q = qh * _sigmoid(qh)
    k = 1.0 - fg
    pad = jnp.zeros((F - 3 * heads, F), _F32)
    cols = jnp.concatenate([fg, k, q, pad], axis=0).T
    v = ih_ref[...]
    outs = []
    for h in range(heads):
        s_new = cols[:, h:h + 1] * s0_ref[h] + cols[:, heads + h:heads + h + 1] * v[h:h + 1, :]
        s_ref[h] = s_new
        outs.append(jnp.sum(cols[:, 2 * heads + h:2 * heads + h + 1] * s_new, axis=0, keepdims=True))
    o = jnp.concatenate(outs, axis=0)
    gh = gh_ref[...]
    ms = jnp.mean(o * o, axis=-1, keepdims=True)
    o_ref[...] = o * lax.rsqrt(ms + EPS) * gn_ref[...] * (gh * _sigmoid(gh))


def _hgrn_step(proj3, lb_raw, gnorm, state, layer, *, heads, group0, name):
    B = proj3.shape[0]
    depth, n_lb = lb_raw.shape
    F, I = state.shape[3], state.shape[4]
    assert F == LANES and I == LANES and 3 * heads <= F

    def grp(g):
        return pl.BlockSpec((None, heads, LANES), lambda b: (b, group0 + g, 0))

    blocks = 2 * (2 * _nbytes((heads, F, I), _F32) + 6 * _nbytes((heads, LANES), _F32))
    return pl.pallas_call(
        functools.partial(_hgrn_step_body, layer=layer, heads=heads),
        grid=(B,),
        in_specs=[pl.BlockSpec((depth, heads, F), lambda b: (0, 0, 0)),
                  pl.BlockSpec((1, I), lambda b: (0, 0)),
                  grp(0), grp(1), grp(2), grp(3),
                  pl.BlockSpec((None, None, heads, F, I), lambda b: (layer, b, 0, 0, 0))],
        out_specs=[pl.BlockSpec((None, heads, I), lambda b: (b, 0, 0)),
                   pl.BlockSpec((None, heads, F, I), lambda b: (b, 0, 0, 0))],
        out_shape=[jax.ShapeDtypeStruct((B, heads, I), _F32),
                   jax.ShapeDtypeStruct((B, heads, F, I), _F32)],
        compiler_params=_compiler_params(("parallel",), blocks + (2 << 20)),
        name=name,
    )(lb_raw.reshape(depth, heads, F), gnorm.reshape(1, I), proj3, proj3, proj3, proj3, state)


def _pick(n, candidates):
    for c in candidates:
        if n % c == 0:
            return c
    return n


def _tiles(M, d_model, d_ff, in_cols):
    tm = _pick(M, (1024, 512, 256, 128, 64, 32, 16))
    return dict(
        tm=tm,
        tn_up=_pick(d_ff, (256, 128)),
        tm_down=min(tm, 512),
        tn_down=_pick(d_model, (512, 256, 128)),
        tn_in=_pick(in_cols, (1024, 512, 256, 128)),
        tn_out=_pick(d_model, (1024, 512, 256, 128)),
        tr=_pick(M, (256, 128, 64, 32, 16)),
    )


def _cast_tiles(d_model, d_ff, in_cols):
    return dict(tn_up=_pick(d_ff, (256, 128)), tn_down=_pick(d_model, (256, 128)),
                tn_in=_pick(in_cols, (512, 256, 128)), tn_out=_pick(d_model, (512, 256, 128)))


def _ffn_blocks(xp, xpb, xs, xsb, wg, wu, wd, gain, bias, layer, alpha, t_p, t_s, tag):
    m_s = xs.shape[0]
    h_s, wg_b, wu_b = _ffn_up_cast(xsb, wg, wu, layer, tn=t_s["tn_up"], name=f"ffn_up_s{tag}")
    a_s, wd_b = _matmul_cast(h_s, wd, layer, tn=t_s["tn_down"], name=f"ffn_down_s{tag}")
    xs, xsb = _residual_ln(xs, a_s, gain, bias, alpha=alpha, scale=0.5, tr=m_s, name=f"ln_s{tag}")
    h_p = _ffn_up(xpb, wg_b, wu_b, tm=t_p["tm"], tn=t_p["tn_up"], name=f"ffn_up_p{tag}")
    a_p = _matmul(h_p, wd_b, tm=t_p["tm_down"], tn=t_p["tn_down"], name=f"ffn_down_p{tag}")
    xp, xpb = _residual_ln(xp, a_p, gain, bias, alpha=alpha, scale=0.5, tr=t_p["tr"], name=f"ln_p{tag}")
    return xp, xpb, xs, xsb


def kernel(x_prompt, x_sample, cache_k, cache_v, state_hgrn, page_table, ln_gain, ln_bias, ffn1_gate, ffn1_up, ffn1_down, w_in, lambda_qk, subln_gain, hgrn_lower_bound, hgrn_gnorm_gain, w_out, ffn2_gate, ffn2_up, ffn2_down):
    batch, seq, d_model = x_prompt.shape
    dec_batch, dec_seq, _ = x_sample.shape
    assert dec_seq == 1
    depth = w_in.shape[0]
    att_heads, dv = cache_v.shape[3], cache_v.shape[4]
    dk = dv // 2
    assert cache_k.shape[4] == dv == LANES
    rot = dk // 4
    hg_heads, hg_f, hg_i = state_hgrn.shape[2:]
    d_ff = ffn1_gate.shape[2]
    in_cols = w_in.shape[2]
    att_width = att_heads * dv
    hg_col0 = 3 * att_width
    alpha = (2 * depth) ** 0.25
    att_scale = dk ** -0.5
    n_past = page_table.shape[1] * cache_k.shape[2]

    m_p = batch * seq
    m_s = 16
    t_p = _tiles(m_p, d_model, d_ff, in_cols)
    t_s = _cast_tiles(d_model, d_ff, in_cols)
    tab_p = _rotary_tables(jnp.arange(seq, dtype=jnp.int32), dk, rot)
    tab_s = _rotary_tables(jnp.full((m_s,), n_past, jnp.int32), dk, rot)
    tq = _pick(seq, (256, 128))

    xp = x_prompt.reshape(m_p, d_model)
    xs = jnp.pad(x_sample.reshape(dec_batch, d_model), ((0, m_s - dec_batch), (0, 0)))
    xpb, xsb = xp.astype(_BF16), xs.astype(_BF16)

    outs = {k: [] for k in ("kp", "vp", "sp", "ks", "vs", "ss")}
    for l in range(depth):
        lambda_init = 0.8 - 0.6 * math.exp(-0.3 * l)
        gains, biases = ln_gain[l], ln_bias[l]

        xp, xpb, xs, xsb = _ffn_blocks(xp, xpb, xs, xsb, ffn1_gate, ffn1_up, ffn1_down, gains[0], biases[0], l,
                                       alpha, t_p, t_s, f"{l}a")

        proj_s, win_b = _matmul_cast(xsb, w_in, l, tn=t_s["tn_in"], name=f"w_in_s{l}")
        q_s, k_s = _rotary(proj_s, tab_s, heads=att_heads, width=dv, half=rot // 2, q_scale=att_scale,
                           tr=m_s, q_dtype=_F32, name=f"rotary_s{l}")
        proj3 = proj_s[:dec_batch].reshape(dec_batch, in_cols // LANES, LANES)
        q3 = q_s[:dec_batch].reshape(dec_batch, att_heads, dv)
        k3 = k_s[:dec_batch].reshape(dec_batch, att_heads, dv)
        v3 = proj_s[:dec_batch, 2 * att_width:3 * att_width].reshape(dec_batch, att_heads, dv)
        att_s = _decode_attention(q3, k3, v3, cache_k, cache_v, l, page_table, lambda_qk[l], subln_gain[l],
                                  dk=dk, lambda_init=lambda_init, n_fetch=_pick(page_table.shape[1], (8, 4, 2, 1)),
                                  name=f"attn_s{l}")
        o_s, st_s = _hgrn_step(proj3, hgrn_lower_bound, hgrn_gnorm_gain[l], state_hgrn, l, heads=hg_heads,
                               group0=hg_col0 // (hg_heads * LANES), name=f"hgrn_s{l}")
        merged_s = jnp.concatenate([att_s.reshape(dec_batch, att_width), o_s.reshape(dec_batch, hg_heads * hg_i)],
                                   axis=1)
        merged_s = jnp.pad(merged_s, ((0, m_s - dec_batch), (0, 0))).astype(_BF16)
        mixed_s, wout_b = _matmul_cast(merged_s, w_out, l, tn=t_s["tn_out"], name=f"w_out_s{l}")
        xs, xsb = _residual_ln(xs, mixed_s, gains[1], biases[1], alpha=alpha, scale=1.0, tr=m_s,
                               name=f"ln_s{l}b")
        outs["ks"].append(k3.reshape(dec_batch, 1, att_heads, dv))
        outs["vs"].append(v3.reshape(dec_batch, 1, att_heads, dv))
        outs["ss"].append(st_s)

        proj = _matmul(xpb, win_b, tm=t_p["tm"], tn=t_p["tn_in"], name=f"w_in_p{l}")
        q_rot, k_rot = _rotary(proj, tab_p, heads=att_heads, width=dv, half=rot // 2, q_scale=att_scale,
                               tr=min(t_p["tr"], seq), q_dtype=_BF16, name=f"rotary_p{l}")
        att = _prompt_attention(q_rot, k_rot, proj, lambda_qk[l], subln_gain[l], batch=batch, seq=seq,
                                heads=att_heads, dk=dk, v_col0=2 * att_width, lambda_init=lambda_init, tq=tq,
                                name=f"attn_p{l}")
        o_h, s_new = _hgrn_prompt(proj, hgrn_lower_bound, hgrn_gnorm_gain[l], l, batch=batch, seq=seq,
                                  heads=hg_heads, col0=hg_col0, heads_per_step=_pick(hg_heads, (4, 2, 1)),
                                  name=f"hgrn_p{l}")
        mixed = _matmul_concat(att, o_h, wout_b, tm=t_p["tm"], tn=t_p["tn_out"], name=f"w_out_p{l}")
        xp, xpb = _residual_ln(xp, mixed, gains[1], biases[1], alpha=alpha, scale=1.0, tr=t_p["tr"],
                               name=f"ln_p{l}b")
        outs["kp"].append(k_rot.reshape(batch, seq, att_heads, dv))
        outs["vp"].append(proj[:, 2 * att_width:3 * att_width].reshape(batch, seq, att_heads, dv))
        outs["sp"].append(s_new)

        xp, xpb, xs, xsb = _ffn_blocks(xp, xpb, xs, xsb, ffn2_gate, ffn2_up, ffn2_down, gains[2], biases[2], l,
                                       alpha, t_p, t_s, f"{l}c")

    return (xp.reshape(batch, seq, d_model), xs[:dec_batch].reshape(dec_batch, 1, d_model),
            jnp.stack(outs["kp"]), jnp.stack(outs["vp"]), jnp.stack(outs["sp"]),
            jnp.stack(outs["ks"]), jnp.stack(outs["vs"]), jnp.stack(outs["ss"]))
```

```python
import functools
import math

import numpy as np
import jax
import jax.numpy as jnp
from jax import lax
from jax.experimental import pallas as pl
from jax.experimental.pallas import tpu as pltpu

_F32 = jnp.float32
_BF16 = jnp.bfloat16

EPS = 1e-5
ROPE_THETA = 500000.0
HGRN_CHUNK = 64
V7X_VMEM_BYTES = 64 * 1024 * 1024
LANES = 128


def _compiler_params(semantics, block_bytes):
    limit = min(int(block_bytes * 1.25) + (8 << 20), V7X_VMEM_BYTES - (6 << 20))
    return pltpu.CompilerParams(dimension_semantics=semantics, vmem_limit_bytes=limit)


def _nbytes(shape, dtype):
    return int(np.prod(shape)) * jnp.dtype(dtype).itemsize


class _Casts:
    def __init__(self, casts, grid):
        steps_j = grid[1]
        self.views, self.in_specs, self.out_specs, self.out_shapes, self.shapes = [], [], [], [], []
        self.bytes = 0
        for w, layer in casts:
            depth, K, N = w.shape
            slabs = max(s for s in range(1, grid[0] * grid[1] + 1) if K % s == 0 and (K // s) % 16 == 0)
            r = K // slabs

            def index(i, j, layer=layer, slabs=slabs):
                return (layer, jnp.minimum(i * steps_j + j, slabs - 1), 0, 0)

            self.views.append(w.reshape(depth, slabs, r, N))
            self.in_specs.append(pl.BlockSpec((None, None, r, N), index))
            self.out_specs.append(pl.BlockSpec((None, r, N), lambda i, j, index=index: index(i, j)[1:]))
            self.out_shapes.append(jax.ShapeDtypeStruct((slabs, r, N), _BF16))
            self.shapes.append((K, N))
            self.bytes += 2 * (_nbytes((r, N), _F32) + _nbytes((r, N), _BF16))

    def __len__(self):
        return len(self.views)

    @staticmethod
    def run(in_refs, out_refs):
        for src, dst in zip(in_refs, out_refs):
            dst[...] = src[...].astype(_BF16)

    def finish(self, outs):
        return [o.reshape(shape) for o, shape in zip(outs, self.shapes)]


def _mm_body(x_ref, w_ref, *rest, n_casts):
    o_ref = rest[n_casts]
    o_ref[...] = jnp.dot(x_ref[...], w_ref[...], preferred_element_type=_F32).astype(o_ref.dtype)
    _Casts.run(rest[:n_casts], rest[n_casts + 1:])


def _matmul(x, w, *, tm, tn, casts=(), out_dtype=_F32, name):
    M, K = x.shape
    N = w.shape[1]
    assert M % tm == 0 and N % tn == 0
    grid = (M // tm, N // tn)
    side = _Casts(casts, grid)
    blocks = 2 * (_nbytes((tm, K), x.dtype) + _nbytes((K, tn), w.dtype) + _nbytes((tm, tn), out_dtype))
    outs = pl.pallas_call(
        functools.partial(_mm_body, n_casts=len(side)),
        grid=grid,
        in_specs=[pl.BlockSpec((tm, K), lambda i, j: (i, 0)),
                  pl.BlockSpec((K, tn), lambda i, j: (0, j))] + side.in_specs,
        out_specs=[pl.BlockSpec((tm, tn), lambda i, j: (i, j))] + side.out_specs,
        out_shape=[jax.ShapeDtypeStruct((M, N), out_dtype)] + side.out_shapes,
        compiler_params=_compiler_params(("arbitrary", "arbitrary"),
                                         blocks + side.bytes + _nbytes((tm, tn), _F32)),
        name=name,
    )(x, w, *side.views)
    return outs[0], side.finish(outs[1:])


def _mm2_body(xa_ref, xb_ref, wa_ref, wb_ref, o_ref):
    o_ref[...] = (jnp.dot(xa_ref[...], wa_ref[...], preferred_element_type=_F32)
                  + jnp.dot(xb_ref[...], wb_ref[...], preferred_element_type=_F32)).astype(o_ref.dtype)


def _matmul_concat(xa, xb, w, *, tm, tn, name):
    M, ka = xa.shape
    kb = xb.shape[1]
    N = w.shape[1]
    assert ka == kb and w.shape[0] == ka + kb and M % tm == 0 and N % tn == 0
    blocks = 2 * (2 * _nbytes((tm, ka), xa.dtype) + 2 * _nbytes((ka, tn), w.dtype) + _nbytes((tm, tn), _F32))
    return pl.pallas_call(
        _mm2_body,
        grid=(M // tm, N // tn),
        in_specs=[pl.BlockSpec((tm, ka), lambda i, j: (i, 0)),
                  pl.BlockSpec((tm, kb), lambda i, j: (i, 0)),
                  pl.BlockSpec((ka, tn), lambda i, j: (0, j)),
                  pl.BlockSpec((kb, tn), lambda i, j: (1, j))],
        out_specs=pl.BlockSpec((tm, tn), lambda i, j: (i, j)),
        out_shape=jax.ShapeDtypeStruct((M, N), _F32),
        compiler_params=_compiler_params(("parallel", "arbitrary"), blocks + 2 * _nbytes((tm, tn), _F32)),
        name=name,
    )(xa, xb, w, w)


def _ffn_up_body(x_ref, wg_ref, wu_ref, *rest, n_casts):
    o_ref = rest[n_casts]
    x = x_ref[...]
    g = jnp.dot(x, wg_ref[...], preferred_element_type=_F32)
    u = jnp.dot(x, wu_ref[...], preferred_element_type=_F32)
    o_ref[...] = (g / (1.0 + jnp.exp(-g)) * u).astype(o_ref.dtype)
    _Casts.run(rest[:n_casts], rest[n_casts + 1:])


def _ffn_up(x, wg, wu, *, tm, tn, casts=(), name):
    M, K = x.shape
    N = wg.shape[1]
    assert M % tm == 0 and N % tn == 0
    grid = (M // tm, N // tn)
    side = _Casts(casts, grid)
    blocks = 2 * (_nbytes((tm, K), x.dtype) + 2 * _nbytes((K, tn), wg.dtype) + _nbytes((tm, tn), _BF16))
    outs = pl.pallas_call(
        functools.partial(_ffn_up_body, n_casts=len(side)),
        grid=grid,
        in_specs=[pl.BlockSpec((tm, K), lambda i, j: (i, 0)),
                  pl.BlockSpec((K, tn), lambda i, j: (0, j)),
                  pl.BlockSpec((K, tn), lambda i, j: (0, j))] + side.in_specs,
        out_specs=[pl.BlockSpec((tm, tn), lambda i, j: (i, j))] + side.out_specs,
        out_shape=[jax.ShapeDtypeStruct((M, N), _BF16)] + side.out_shapes,
        compiler_params=_compiler_params(("arbitrary", "arbitrary"),
                                         blocks + side.bytes + 3 * _nbytes((tm, tn), _F32)),
        name=name,
    )(x, wg, wu, *side.views)
    return outs[0], side.finish(outs[1:])


def _ffn_up_cast_body(x_ref, wg_ref, wu_ref, o_ref, wgb_ref, wub_ref):
    x = x_ref[...]
    wg = wg_ref[...].astype(_BF16)
    wu = wu_ref[...].astype(_BF16)
    wgb_ref[...] = wg
    wub_ref[...] = wu
    g = jnp.dot(x, wg, preferred_element_type=_F32)
    u = jnp.dot(x, wu, preferred_element_type=_F32)
    o_ref[...] = (g / (1.0 + jnp.exp(-g)) * u).astype(o_ref.dtype)


def _ffn_up_cast(x, wg, wu, layer, *, tn, name):
    M, K = x.shape
    N = wg.shape[2]
    assert N % tn == 0
    blocks = 2 * (_nbytes((M, K), x.dtype) + 2 * _nbytes((K, tn), _F32) + 2 * _nbytes((K, tn), _BF16)
                  + _nbytes((M, tn), _BF16))
    w_f32 = pl.BlockSpec((None, K, tn), lambda j: (layer, 0, j))
    w_b16 = pl.BlockSpec((K, tn), lambda j: (0, j))
    return pl.pallas_call(
        _ffn_up_cast_body,
        grid=(N // tn,),
        in_specs=[pl.BlockSpec((M, K), lambda j: (0, 0)), w_f32, w_f32],
        out_specs=[pl.BlockSpec((M, tn), lambda j: (0, j)), w_b16, w_b16],
        out_shape=[jax.ShapeDtypeStruct((M, N), _BF16), jax.ShapeDtypeStruct((K, N), _BF16),
                   jax.ShapeDtypeStruct((K, N), _BF16)],
        compiler_params=_compiler_params(("parallel",), blocks + 2 * _nbytes((K, tn), _BF16)),
        name=name,
    )(x, wg, wu)


def _ln_body(x_ref, a_ref, g_ref, b_ref, y_ref, yb_ref, *, alpha, scale):
    z = alpha * x_ref[...] + scale * a_ref[...]
    mu = jnp.mean(z, axis=-1, keepdims=True)
    zc = z - mu
    var = jnp.mean(zc * zc, axis=-1, keepdims=True)
    y = zc * lax.rsqrt(var + EPS) * g_ref[...] + b_ref[...]
    y_ref[...] = y
    yb_ref[...] = y.astype(_BF16)


def _residual_ln(x, a, gain, bias, *, alpha, scale, tr, name):
    M, D = x.shape
    assert M % tr == 0
    blocks = 2 * (3 * _nbytes((tr, D), _F32) + _nbytes((tr, D), _BF16))
    row = pl.BlockSpec((tr, D), lambda i: (i, 0))
    vec = pl.BlockSpec((1, D), lambda i: (0, 0))
    return pl.pallas_call(
        functools.partial(_ln_body, alpha=alpha, scale=scale),
        grid=(M // tr,),
        in_specs=[row, row, vec, vec],
        out_specs=[row, row],
        out_shape=[jax.ShapeDtypeStruct((M, D), _F32), jax.ShapeDtypeStruct((M, D), _BF16)],
        compiler_params=_compiler_params(("parallel",), blocks + 2 * _nbytes((tr, D), _F32)),
        name=name,
    )(x, a, gain.reshape(1, D), bias.reshape(1, D))


def _rotary_tables(pos, dk, rot):
    half = rot // 2
    inv_freq = jnp.exp(-math.log(ROPE_THETA) * jnp.arange(half, dtype=_F32) / half)
    ang = pos.astype(_F32)[:, None] * inv_freq[None, :]
    cos, sin = jnp.cos(ang), jnp.sin(ang)
    n = pos.shape[0]
    ones = jnp.ones((n, dk - rot), _F32)
    zeros_h = jnp.zeros((n, half), _F32)
    zeros_r = jnp.zeros((n, dk - rot), _F32)
    c = jnp.concatenate([cos, cos, ones], axis=1)
    s_hi = jnp.concatenate([-sin, zeros_h, zeros_r], axis=1)
    s_lo = jnp.concatenate([zeros_h, sin, zeros_r], axis=1)
    return tuple(jnp.concatenate([t, t], axis=1) for t in (c, s_hi, s_lo))


def _rotary_body(q_ref, k_ref, c_ref, shi_ref, slo_ref, qo_ref, ko_ref, *, heads, width, half, q_scale):
    c, s_hi, s_lo = c_ref[...], shi_ref[...], slo_ref[...]
    for h in range(heads):
        cols = slice(h * width, (h + 1) * width)
        for src, dst, scale in ((q_ref, qo_ref, q_scale), (k_ref, ko_ref, None)):
            x = src[:, cols]
            y = x * c + pltpu.roll(x, width - half, 1) * s_hi + pltpu.roll(x, half, 1) * s_lo
            if scale is not None:
                y = y * scale
            dst[:, cols] = y.astype(dst.dtype)


def _rotary(proj, tables, *, heads, width, half, q_scale, tr, q_dtype, name):
    M = proj.shape[0]
    W = heads * width
    n_tab = tables[0].shape[0] // tr
    blocks = 2 * (2 * _nbytes((tr, W), _F32) + 3 * _nbytes((tr, width), _F32)
                  + _nbytes((tr, W), q_dtype) + _nbytes((tr, W), _F32))
    tab = pl.BlockSpec((tr, width), lambda i: (i % n_tab, 0))
    return pl.pallas_call(
        functools.partial(_rotary_body, heads=heads, width=width, half=half, q_scale=q_scale),
        grid=(M // tr,),
        in_specs=[pl.BlockSpec((tr, W), lambda i: (i, 0)), pl.BlockSpec((tr, W), lambda i: (i, 1)),
                  tab, tab, tab],
        out_specs=[pl.BlockSpec((tr, W), lambda i: (i, 0)), pl.BlockSpec((tr, W), lambda i: (i, 0))],
        out_shape=[jax.ShapeDtypeStruct((M, W), q_dtype), jax.ShapeDtypeStruct((M, W), _F32)],
        compiler_params=_compiler_params(("parallel",), blocks),
        name=name,
    )(proj, proj, *tables)


def _lambda_value(lq, lambda_init):
    s01 = jnp.sum(lq[0:1, :] * lq[1:2, :], axis=1, keepdims=True)
    s23 = jnp.sum(lq[2:3, :] * lq[3:4, :], axis=1, keepdims=True)
    return jnp.exp(s01) - jnp.exp(s23) + lambda_init


def _sub_rms(x, gain, lambda_init):
    ms = jnp.mean(x * x, axis=-1, keepdims=True)
    return x * lax.rsqrt(ms + EPS) * gain * (1.0 - lambda_init)


def _split_maps(q, dk):
    lane = lax.broadcasted_iota(jnp.int32, q.shape, 1)
    zero = jnp.zeros_like(q)
    return jnp.concatenate([jnp.where(lane < dk, q, zero), jnp.where(lane >= dk, q, zero)], axis=0)


def _prompt_attn_body(lq_ref, sg_ref, q_ref, k_ref, v_ref, o_ref, kb_ref, vt_ref, s_ref, p_ref, acc_ref, *,
                      tq, dk, lambda_init):
    qi = pl.program_id(2)
    n_blocks = k_ref.shape[0] // tq

    @pl.when(qi == 0)
    def _stage():
        for blk in range(n_blocks):
            rows = slice(blk * tq, (blk + 1) * tq)
            kb_ref[rows, :] = k_ref[rows, :].astype(_BF16)
            vt_ref[:, rows] = v_ref[rows, :].T.astype(_BF16)

    qt = q_ref[...].astype(_F32).T
    sub = lax.broadcasted_iota(jnp.int32, qt.shape, 0)
    qs = jnp.concatenate([jnp.where(sub < dk, qt, 0.0), jnp.where(sub >= dk, qt, 0.0)], axis=1).astype(_BF16)

    def scores(j):
        k = kb_ref[pl.ds(pl.multiple_of(j * tq, tq), tq), :]
        return jnp.dot(k, qs, preferred_element_type=_F32)

    def weighted_values(j, p):
        vt = vt_ref[:, pl.ds(pl.multiple_of(j * tq, tq), tq)]
        return jnp.dot(vt, p, preferred_element_type=_F32)

    def softmax_update(m, l, slot, masked):
        parts = []
        for c in range(0, 2 * tq, LANES):
            strip = slice(c, c + LANES)
            s = s_ref[slot, :, strip]
            if masked:
                key = lax.broadcasted_iota(jnp.int32, s.shape, 0)
                qry = lax.broadcasted_iota(jnp.int32, s.shape, 1) + (c % tq)
                s = jnp.where(key <= qry, s, -jnp.inf)
            m_new = jnp.maximum(m[:, strip], jnp.max(s, axis=0, keepdims=True))
            a = jnp.exp(m[:, strip] - m_new)
            p = jnp.exp(s - m_new)
            p_ref[slot, :, strip] = p.astype(_BF16)
            parts.append((m_new, a * l[:, strip] + jnp.sum(p, axis=0, keepdims=True), a))
        return tuple(jnp.concatenate(x, axis=1) for x in zip(*parts))

    def trip(j, carry):
        m, l, a_prev = carry
        cur = lax.rem(j, 2)
        pv = weighted_values(jnp.maximum(j - 1, 0), p_ref[1 - cur])
        m, l, a = softmax_update(m, l, cur, False)
        s_ref[1 - cur] = scores(j + 1)
        acc_ref[...] = a_prev * acc_ref[...] + pv
        return m, l, a

    s_ref[0] = scores(0)
    p_ref[1] = jnp.zeros(p_ref.shape[1:], _BF16)
    acc_ref[...] = jnp.zeros(acc_ref.shape, _F32)
    init = (jnp.full((1, 2 * tq), -jnp.inf, _F32), jnp.zeros((1, 2 * tq), _F32), jnp.ones((1, 2 * tq), _F32))
    m, l, a_prev = lax.fori_loop(0, qi, trip, init)
    cur = lax.rem(qi, 2)
    acc = a_prev * acc_ref[...] + weighted_values(jnp.maximum(qi - 1, 0), p_ref[1 - cur])
    m, l, a = softmax_update(m, l, cur, True)
    acc = a * acc + weighted_values(qi, p_ref[cur])
    o = acc / l
    lam = _lambda_value(lq_ref[...], lambda_init)
    out = o[:, :tq] - lam * o[:, tq:]
    ms = jnp.mean(out * out, axis=0, keepdims=True)
    out = out * lax.rsqrt(ms + EPS) * sg_ref[...] * (1.0 - lambda_init)
    o_ref[...] = out.T.astype(o_ref.dtype)


def _prompt_attention(q, k, proj, lq, subln, *, batch, seq, heads, dk, v_col0, lambda_init, tq, name):
    M = q.shape[0]
    dv = 2 * dk
    nq = seq // tq
    blocks = 2 * (_nbytes((tq, dv), _BF16) * 2 + 2 * _nbytes((seq, dv), _F32))
    scratch = 6 * _nbytes((2 * tq, tq), _F32)
    return pl.pallas_call(
        functools.partial(_prompt_attn_body, tq=tq, dk=dk, lambda_init=lambda_init),
        grid=(batch, heads, nq),
        in_specs=[pl.BlockSpec((4, dk), lambda b, h, i: (0, 0)),
                  pl.BlockSpec((dv, 1), lambda b, h, i: (0, 0)),
                  pl.BlockSpec((tq, dv), lambda b, h, i: (b * nq + i, h)),
                  pl.BlockSpec((seq, dv), lambda b, h, i: (b, h)),
                  pl.BlockSpec((seq, dv), lambda b, h, i: (b, v_col0 // dv + h))],
        out_specs=pl.BlockSpec((tq, dv), lambda b, h, i: (b * nq + i, h)),
        out_shape=jax.ShapeDtypeStruct((M, heads * dv), _BF16),
        scratch_shapes=[pltpu.VMEM((seq, dv), _BF16), pltpu.VMEM((dv, seq), _BF16),
                        pltpu.VMEM((2, tq, 2 * tq), _F32), pltpu.VMEM((2, tq, 2 * tq), _BF16),
                        pltpu.VMEM((dv, 2 * tq), _F32)],
        compiler_params=_compiler_params(("parallel", "parallel", "arbitrary"), blocks + scratch),
        name=name,
    )(lq, subln.reshape(dv, 1), q, k, proj)


def _decode_attn_body(pt_ref, lq_ref, sg_ref, q_ref, kn_ref, vn_ref, *rest, n_fetch, heads, dk, lambda_init):
    k_refs = rest[:n_fetch]
    v_refs = rest[n_fetch:2 * n_fetch]
    o_ref = rest[2 * n_fetch]
    qm_ref, bias_ref, m_ref, l_ref, acc_ref = rest[2 * n_fetch + 1:]
    j = pl.program_id(1)

    @pl.when(j == 0)
    def _init():
        qm_ref[...] = _split_maps(q_ref[...], dk).astype(_BF16)
        row = lax.broadcasted_iota(jnp.int32, bias_ref.shape, 0)
        col = lax.broadcasted_iota(jnp.int32, bias_ref.shape, 1)
        bias_ref[...] = jnp.where((row & (heads - 1)) == (col & (heads - 1)), 0.0, -jnp.inf)
        m_ref[...] = jnp.full(m_ref.shape, -jnp.inf, _F32)
        l_ref[...] = jnp.zeros(l_ref.shape, _F32)
        acc_ref[...] = jnp.zeros(acc_ref.shape, _F32)

    qm = qm_ref[...]
    for k_ref, v_ref in zip(k_refs, v_refs):
        k = k_ref[...].astype(_BF16)
        v = v_ref[...].astype(_BF16)
        s = lax.dot_general(qm, k, (((1,), (1,)), ((), ())), preferred_element_type=_F32) + bias_ref[...]
        m = m_ref[...]
        m_new = jnp.maximum(m, jnp.max(s, axis=1, keepdims=True))
        a = jnp.exp(m - m_new)
        p = jnp.exp(s - m_new)
        l_ref[...] = a * l_ref[...] + jnp.sum(p, axis=1, keepdims=True)
        acc_ref[...] = a * acc_ref[...] + jnp.dot(p.astype(_BF16), v, preferred_element_type=_F32)
        m_ref[...] = m_new

    @pl.when(j == pl.num_programs(1) - 1)
    def _finish():
        prod = _split_maps(q_ref[...] * kn_ref[...], dk)
        s_self = jnp.sum(prod, axis=1, keepdims=True)
        v_self = jnp.concatenate([vn_ref[...], vn_ref[...]], axis=0)
        m = m_ref[...]
        m_new = jnp.maximum(m, s_self)
        a = jnp.exp(m - m_new)
        p = jnp.exp(s_self - m_new)
        l = a * l_ref[...] + p
        o = (a * acc_ref[...] + p * v_self) / l
        lam = _lambda_value(lq_ref[...], lambda_init)
        out = o[:heads] - lam * o[heads:]
        o_ref[...] = _sub_rms(out, sg_ref[...], lambda_init)


def _decode_attention(q, k_new, v_new, cache_k, cache_v, layer, page_table, lq, subln, *, dk, lambda_init,
                      n_fetch, name):
    B, heads, dv = q.shape
    depth, n_pool, page = cache_k.shape[:3]
    n_pages = page_table.shape[1]
    assert n_pages % n_fetch == 0 and heads & (heads - 1) == 0
    rows = page * heads
    ck = cache_k.reshape(depth, n_pool, rows, dv)
    cv = cache_v.reshape(depth, n_pool, rows, dv)

    def page_spec(r):
        return pl.BlockSpec((None, None, rows, dv),
                            lambda b, j, pt: (layer, pt[b * n_pages + j * n_fetch + r], 0, 0))

    head_spec = pl.BlockSpec((None, heads, dv), lambda b, j, pt: (b, 0, 0))
    blocks = 2 * (2 * n_fetch * _nbytes((rows, dv), _F32) + 4 * _nbytes((heads, dv), _F32))
    scratch = 4 * _nbytes((2 * heads, rows), _F32) + 2 * _nbytes((rows, dv), _BF16)
    grid_spec = pltpu.PrefetchScalarGridSpec(
        num_scalar_prefetch=1,
        grid=(B, n_pages // n_fetch),
        in_specs=[pl.BlockSpec((4, dk), lambda b, j, pt: (0, 0)),
                  pl.BlockSpec((1, dv), lambda b, j, pt: (0, 0)),
                  head_spec, head_spec, head_spec]
                 + [page_spec(r) for r in range(n_fetch)] + [page_spec(r) for r in range(n_fetch)],
        out_specs=head_spec,
        scratch_shapes=[pltpu.VMEM((2 * heads, dv), _BF16), pltpu.VMEM((2 * heads, rows), _F32),
                        pltpu.VMEM((2 * heads, 1), _F32), pltpu.VMEM((2 * heads, 1), _F32),
                        pltpu.VMEM((2 * heads, dv), _F32)],
    )
    return pl.pallas_call(
        functools.partial(_decode_attn_body, n_fetch=n_fetch, heads=heads, dk=dk, lambda_init=lambda_init),
        grid_spec=grid_spec,
        out_shape=jax.ShapeDtypeStruct((B, heads, dv), _F32),
        compiler_params=_compiler_params(("parallel", "arbitrary"), blocks + scratch),
        name=name,
    )(page_table.reshape(-1), lq, subln.reshape(1, dv), q, k_new, v_new,
      *([ck] * n_fetch), *([cv] * n_fetch))


def _hgrn_constants(c):
    t = np.arange(c)
    tt, uu = t[:, None], t[None, :]
    masks = []
    s = c // 2
    while s >= 1:
        x = tt ^ uu
        wide = np.zeros((c, 2 * c), np.float32)
        off = (len(masks) % 2) * c
        wide[:, off:off + c] = (x >= s) & (x < 2 * s) & (tt > uu)
        masks.append(wide)
        s //= 2
    return jnp.asarray(np.stack(masks))


def _level_reference(b, s):
    n = b.shape[0]
    if 2 * s >= 8:
        pieces = [jnp.broadcast_to(b[base + s - 1:base + s, :], (2 * s, b.shape[1])) for base in range(0, n, 2 * s)]
        return jnp.concatenate(pieces, axis=0)
    pos = lax.broadcasted_iota(jnp.int32, b.shape, 0) & (2 * s - 1)
    ref = b
    for p in range(2 * s):
        shift = p - (s - 1)
        if shift != 0:
            ref = jnp.where(pos == p, pltpu.roll(b, shift % n, 0), ref)
    return ref


def _sigmoid(x):
    return 1.0 / (1.0 + jnp.exp(-x))


def _lower_bound(rows, layer):
    top = functools.reduce(jnp.maximum, rows)
    e = [jnp.exp(r - top) for r in rows]
    total = functools.reduce(lambda a, b: a + b, e)
    lb = jnp.zeros_like(total)
    for j in range(1, layer + 1):
        lb = lb + e[j] / total
    return lb


def _hgrn_prompt_body(mask_ref, lbraw_ref, gn_ref, qh_ref, fh_ref, ih_ref, gh_ref, o_ref, s_ref, st_ref, *,
                      layer, chunk, n_chunks):
    n_levels = mask_ref.shape[0]
    width = LANES
    n_heads = st_ref.shape[0]
    lb_all = _lower_bound([lbraw_ref[j:j + 1, :] for j in range(lbraw_ref.shape[0])], layer)
    gn = gn_ref[...]
    row = lax.broadcasted_iota(jnp.int32, (chunk, width), 0)
    st_ref[...] = jnp.zeros(st_ref.shape, _F32)

    def head_chunk(ci, hh):
        rows = pl.ds(pl.multiple_of(ci * chunk, chunk), chunk)
        cols = slice(hh * width, (hh + 1) * width)
        lb = lb_all[:, cols]
        qh, fh, v, gh = qh_ref[rows, cols], fh_ref[rows, cols], ih_ref[rows, cols], gh_ref[rows, cols]
        fg = lb + (1.0 - lb) * _sigmoid(fh)
        q = qh * _sigmoid(qh)
        k = 1.0 - fg
        b = jnp.log(fg)
        d = 1
        while d < chunk:
            b = b + jnp.where(row >= d, pltpu.roll(b, d, 0), 0.0)
            d *= 2
        b_last = b[chunk - 1:chunk]

        def level_operands(s):
            upper = (row & s) != 0
            gap = b - _level_reference(b, s)
            dec = jnp.exp(jnp.where(upper, gap, -gap))
            return jnp.where(upper, q * dec, 0.0), jnp.where(upper, 0.0, k * dec)

        attn = jnp.zeros((chunk, 2 * chunk), _F32)
        for lvl in range(0, n_levels, 2):
            qa, ka = level_operands(chunk >> (lvl + 1))
            qb, kb = level_operands(chunk >> (lvl + 2))
            r = lax.dot_general(jnp.concatenate([qa, qb], axis=0).astype(_BF16),
                                jnp.concatenate([ka, kb], axis=0).astype(_BF16),
                                (((1,), (1,)), ((), ())), preferred_element_type=_F32)
            attn = attn + r[:chunk] * mask_ref[lvl] + r[chunk:] * mask_ref[lvl + 1]
        vt2 = jnp.concatenate([v, v], axis=0).T.astype(_BF16)
        st = st_ref[hh]
        lhs = jnp.concatenate([(q * jnp.exp(b)).astype(_BF16), attn.astype(_BF16)], axis=1)
        rhs = jnp.concatenate([st.astype(_BF16), vt2], axis=1)
        o = lax.dot_general(lhs, rhs, (((1,), (1,)), ((), ())), preferred_element_type=_F32)
        o = o + jnp.sum(q * k, axis=1, keepdims=True) * v
        k_dec = (k * jnp.exp(b_last - b)).astype(_BF16)
        st_ref[hh] = st * jnp.exp(b_last) + jnp.dot(vt2[:, :chunk], k_dec, preferred_element_type=_F32)
        ms = jnp.mean(o * o, axis=-1, keepdims=True)
        o_ref[rows, cols] = (o * lax.rsqrt(ms + EPS) * gn * (gh * _sigmoid(gh))).astype(o_ref.dtype)

    def body(ci, _):
        for hh in range(n_heads):
            head_chunk(ci, hh)
        return 0

    lax.fori_loop(0, n_chunks, body, 0)
    for hh in range(n_heads):
        s_ref[hh] = st_ref[hh].T


def _hgrn_prompt(proj, lb_raw, gnorm, layer, *, batch, seq, heads, col0, heads_per_step, name):
    M = proj.shape[0]
    depth, n_lb = lb_raw.shape
    F = n_lb // heads
    I = gnorm.shape[0]
    hp = heads_per_step
    assert F == LANES and I == LANES and heads % hp == 0 and col0 % (hp * LANES) == 0
    chunk = math.gcd(seq, HGRN_CHUNK)
    masks = _hgrn_constants(chunk)
    assert masks.shape[0] % 2 == 0 and 2 * chunk == LANES
    wide = hp * LANES

    def col(group):
        return pl.BlockSpec((seq, wide), lambda b, h: (b, (col0 + group * heads * LANES) // wide + h))

    blocks = 2 * (4 * _nbytes((seq, wide), _F32) + _nbytes((seq, wide), _BF16)
                  + _nbytes(masks.shape, _F32) + _nbytes((hp, F, I), _F32))
    return pl.pallas_call(
        functools.partial(_hgrn_prompt_body, layer=layer, chunk=chunk, n_chunks=seq // chunk),
        grid=(batch, heads // hp),
        in_specs=[pl.BlockSpec(masks.shape, lambda b, h: (0, 0, 0)),
                  pl.BlockSpec((depth, wide), lambda b, h: (0, h)),
                  pl.BlockSpec((1, I), lambda b, h: (0, 0)),
                  col(0), col(1), col(2), col(3)],
        out_specs=[pl.BlockSpec((seq, wide), lambda b, h: (b, h)),
                   pl.BlockSpec((None, hp, F, I), lambda b, h: (b, h, 0, 0))],
        out_shape=[jax.ShapeDtypeStruct((M, heads * I), _BF16),
                   jax.ShapeDtypeStruct((batch, heads, F, I), _F32)],
        scratch_shapes=[pltpu.VMEM((hp, I, F), _F32)],
        compiler_params=_compiler_params(("parallel", "parallel"), blocks + (4 << 20)),
        name=name,
    )(masks, lb_raw, gnorm.reshape(1, I), proj, proj, proj, proj)


def _hgrn_step_body(lbraw_ref, gn_ref, qh_ref, fh_ref, ih_ref, gh_ref, s0_ref, o_ref, s_ref, *, layer, heads):
    F = fh_ref.shape[1]
    lb = _lower_bound([lbraw_ref[j] for j in range(lbraw_ref.shape[0])], layer)
    fg = lb + (1.0 - lb) * _sigmoid(fh_ref[...])
    qh = qh_ref[...]
    q = qh * _sigmoid(qh)
    k = 1.0 - fg
    pad = jnp.zeros((F - 3 * heads, F), _F32)
    cols = jnp.concatenate([fg, k, q, pad], axis=0).T
    v = ih_ref[...]
    outs = []
    for h in range(heads):
        s_new = cols[:, h:h + 1] * s0_ref[h] + cols[:, heads + h:heads + h + 1] * v[h:h + 1, :]
        s_ref[h] = s_new
        outs.append(jnp.sum(cols[:, 2 * heads + h:2 * heads + h + 1] * s_new, axis=0, keepdims=True))
    o = jnp.concatenate(outs, axis=0)
    gh = gh_ref[...]
    ms = jnp.mean(o * o, axis=-1, keepdims=True)
    o_ref[...] = o * lax.rsqrt(ms + EPS) * gn_ref[...] * (gh * _sigmoid(gh))


def _hgrn_step(proj3, lb_raw, gnorm, state, layer, *, heads, group0, name):
    B = proj3.shape[0]
    depth, n_lb = lb_raw.shape
    F, I = state.shape[3], state.shape[4]
    assert F == LANES and I == LANES and 3 * heads <= F

    def grp(g):
        return pl.BlockSpec((None, heads, LANES), lambda b: (b, group0 + g, 0))

    blocks = 2 * (2 * _nbytes((heads, F, I), _F32) + 6 * _nbytes((heads, LANES), _F32))
    return pl.pallas_call(
        functools.partial(_hgrn_step_body, layer=layer, heads=heads),
        grid=(B,),
        in_specs=[pl.BlockSpec((depth, heads, F), lambda b: (0, 0, 0)),
                  pl.BlockSpec((1, I), lambda b: (0, 0)),
                  grp(0), grp(1), grp(2), grp(3),
                  pl.BlockSpec((None, None, heads, F, I), lambda b: (layer, b, 0, 0, 0))],
        out_specs=[pl.BlockSpec((None, heads, I), lambda b: (b, 0, 0)),
                   pl.BlockSpec((None, heads, F, I), lambda b: (b, 0, 0, 0))],
        out_shape=[jax.ShapeDtypeStruct((B, heads, I), _F32),
                   jax.ShapeDtypeStruct((B, heads, F, I), _F32)],
        compiler_params=_compiler_params(("parallel",), blocks + (2 << 20)),
        name=name,
    )(lb_raw.reshape(depth, heads, F), gnorm.reshape(1, I), proj3, proj3, proj3, proj3, state)


def _pick(n, candidates):
    for c in candidates:
        if n % c == 0:
            return c
    return n


def _tiles(M, d_model, d_ff, in_cols):
    tm = _pick(M, (1024, 512, 256, 128, 64, 32, 16))
    return dict(
        tm=tm,
        tn_up=_pick(d_ff, (256, 128)),
        tm_down=min(tm, 512),
        tn_down=_pick(d_model, (512, 256, 128)),
        tn_in=_pick(in_cols, (512, 256, 128)),
        tn_out=_pick(d_model, (1024, 512, 256, 128)),
        tr=_pick(M, (256, 128, 64, 32, 16)),
    )


def _decode_tiles(m_s, d_model, d_ff, in_cols):
    return dict(tm=m_s, tn_up=_pick(d_ff, (256, 128)), tn_down=_pick(d_model, (512, 256, 128)),
                tn_in=_pick(in_cols, (1024, 512, 256, 128)), tn_out=_pick(d_model, (1024, 512, 256, 128)))


def _ffn_blocks(xp, xpb, xs, xsb, wg_b, wu_b, wd_cast, gain, bias, alpha, t_p, t_s, tag, *, h_s=None,
                up_casts=(), down_casts=()):
    h_p, cast = _ffn_up(xpb, wg_b, wu_b, tm=t_p["tm"], tn=t_p["tn_up"], casts=[wd_cast, *up_casts],
                        name=f"ffn_up_p{tag}")
    wd_b = cast[0]
    if h_s is None:
        h_s, _ = _ffn_up(xsb, wg_b, wu_b, tm=t_s["tm"], tn=t_s["tn_up"], name=f"ffn_up_s{tag}")
    a_s, _ = _matmul(h_s, wd_b, tm=t_s["tm"], tn=t_s["tn_down"], name=f"ffn_down_s{tag}")
    xs, xsb = _residual_ln(xs, a_s, gain, bias, alpha=alpha, scale=0.5, tr=t_s["tm"], name=f"ln_s{tag}")
    a_p, down_cast = _matmul(h_p, wd_b, tm=t_p["tm_down"], tn=t_p["tn_down"], casts=down_casts,
                             name=f"ffn_down_p{tag}")
    xp, xpb = _residual_ln(xp, a_p, gain, bias, alpha=alpha, scale=0.5, tr=t_p["tr"], name=f"ln_p{tag}")
    return xp, xpb, xs, xsb, cast[1:], down_cast


def kernel(x_prompt, x_sample, cache_k, cache_v, state_hgrn, page_table, ln_gain, ln_bias, ffn1_gate, ffn1_up, ffn1_down, w_in, lambda_qk, subln_gain, hgrn_lower_bound, hgrn_gnorm_gain, w_out, ffn2_gate, ffn2_up, ffn2_down):
    batch, seq, d_model = x_prompt.shape
    dec_batch, dec_seq, _ = x_sample.shape
    assert dec_seq == 1
    depth = w_in.shape[0]
    att_heads, dv = cache_v.shape[3], cache_v.shape[4]
    dk = dv // 2
    assert cache_k.shape[4] == dv == LANES
    rot = dk // 4
    hg_heads, hg_f, hg_i = state_hgrn.shape[2:]
    d_ff = ffn1_gate.shape[2]
    in_cols = w_in.shape[2]
    att_width = att_heads * dv
    hg_col0 = 3 * att_width
    alpha = (2 * depth) ** 0.25
    att_scale = dk ** -0.5
    n_past = page_table.shape[1] * cache_k.shape[2]

    m_p = batch * seq
    m_s = 16
    t_p = _tiles(m_p, d_model, d_ff, in_cols)
    t_s = _decode_tiles(m_s, d_model, d_ff, in_cols)
    tab_p = _rotary_tables(jnp.arange(seq, dtype=jnp.int32), dk, rot)
    tab_s = _rotary_tables(jnp.full((m_s,), n_past, jnp.int32), dk, rot)
    tq = _pick(seq, (256, 128))

    xp = x_prompt.reshape(m_p, d_model)
    xs = jnp.pad(x_sample.reshape(dec_batch, d_model), ((0, m_s - dec_batch), (0, 0)))
    xpb, xsb = xp.astype(_BF16), xs.astype(_BF16)

    h_s0, wg_b, wu_b = _ffn_up_cast(xsb, ffn1_gate, ffn1_up, 0, tn=t_s["tn_up"], name="ffn_up_s0a")

    outs = {k: [] for k in ("kp", "vp", "sp", "ks", "vs", "ss")}
    for l in range(depth):
        lambda_init = 0.8 - 0.6 * math.exp(-0.3 * l)
        gains, biases = ln_gain[l], ln_bias[l]

        xp, xpb, xs, xsb, (win_b,), (wout_b,) = _ffn_blocks(
            xp, xpb, xs, xsb, wg_b, wu_b, (ffn1_down, l), gains[0], biases[0], alpha, t_p, t_s, f"{l}a",
            h_s=h_s0 if l == 0 else None, up_casts=[(w_in, l)], down_casts=[(w_out, l)])

        proj_s, _ = _matmul(xsb, win_b, tm=t_s["tm"], tn=t_s["tn_in"], name=f"w_in_s{l}")
        q_s, k_s = _rotary(proj_s, tab_s, heads=att_heads, width=dv, half=rot // 2, q_scale=att_scale,
                           tr=m_s, q_dtype=_F32, name=f"rotary_s{l}")
        proj3 = proj_s[:dec_batch].reshape(dec_batch, in_cols // LANES, LANES)
        q3 = q_s[:dec_batch].reshape(dec_batch, att_heads, dv)
        k3 = k_s[:dec_batch].reshape(dec_batch, att_heads, dv)
        v3 = proj_s[:dec_batch, 2 * att_width:3 * att_width].reshape(dec_batch, att_heads, dv)
        att_s = _decode_attention(q3, k3, v3, cache_k, cache_v, l, page_table, lambda_qk[l], subln_gain[l],
                                  dk=dk, lambda_init=lambda_init, n_fetch=_pick(page_table.shape[1], (8, 4, 2, 1)),
                                  name=f"attn_s{l}")
        o_s, st_s = _hgrn_step(proj3, hgrn_lower_bound, hgrn_gnorm_gain[l], state_hgrn, l, heads=hg_heads,
                               group0=hg_col0 // (hg_heads * LANES), name=f"hgrn_s{l}")
        merged_s = jnp.concatenate([att_s.reshape(dec_batch, att_width), o_s.reshape(dec_batch, hg_heads * hg_i)],
                                   axis=1)
        merged_s = jnp.pad(merged_s, ((0, m_s - dec_batch), (0, 0))).astype(_BF16)
        mixed_s, _ = _matmul(merged_s, wout_b, tm=t_s["tm"], tn=t_s["tn_out"], name=f"w_out_s{l}")
        xs, xsb = _residual_ln(xs, mixed_s, gains[1], biases[1], alpha=alpha, scale=1.0, tr=m_s,
                               name=f"ln_s{l}b")
        outs["ks"].append(k3.reshape(dec_batch, 1, att_heads, dv))
        outs["vs"].append(v3.reshape(dec_batch, 1, att_heads, dv))
        outs["ss"].append(st_s)

        proj, (wg_c, wu_c) = _matmul(xpb, win_b, tm=t_p["tm"], tn=t_p["tn_in"],
                                     casts=[(ffn2_gate, l), (ffn2_up, l)], name=f"w_in_p{l}")
        q_rot, k_rot = _rotary(proj, tab_p, heads=att_heads, width=dv, half=rot // 2, q_scale=att_scale,
                               tr=min(t_p["tr"], seq), q_dtype=_BF16, name=f"rotary_p{l}")
        att = _prompt_attention(q_rot, k_rot, proj, lambda_qk[l], subln_gain[l], batch=batch, seq=seq,
                                heads=att_heads, dk=dk, v_col0=2 * att_width, lambda_init=lambda_init, tq=tq,
                                name=f"attn_p{l}")
        o_h, s_new = _hgrn_prompt(proj, hgrn_lower_bound, hgrn_gnorm_gain[l], l, batch=batch, seq=seq,
                                  heads=hg_heads, col0=hg_col0, heads_per_step=_pick(hg_heads, (4, 2, 1)),
                                  name=f"hgrn_p{l}")
        mixed = _matmul_concat(att, o_h, wout_b, tm=t_p["tm"], tn=t_p["tn_out"], name=f"w_out_p{l}")
        xp, xpb = _residual_ln(xp, mixed, gains[1], biases[1], alpha=alpha, scale=1.0, tr=t_p["tr"],
                               name=f"ln_p{l}b")
        outs["kp"].append(k_rot.reshape(batch, seq, att_heads, dv))
        outs["vp"].append(proj[:, 2 * att_width:3 * att_width].reshape(batch, seq, att_heads, dv))
        outs["sp"].append(s_new)

        ahead = [(ffn1_gate, l + 1), (ffn1_up, l + 1)] if l + 1 < depth else []
        xp, xpb, xs, xsb, nxt, _ = _ffn_blocks(
            xp, xpb, xs, xsb, wg_c, wu_c, (ffn2_down, l), gains[2], biases[2], alpha, t_p, t_s, f"{l}c",
            up_casts=ahead)
        if ahead:
            wg_b, wu_b = nxt

    return (xp.reshape(batch, seq, d_model), xs[:dec_batch].reshape(dec_batch, 1, d_model),
            jnp.stack(outs["kp"]), jnp.stack(outs["vp"]), jnp.stack(outs["sp"]),
            jnp.stack(outs["ks"]), jnp.stack(outs["vs"]), jnp.stack(outs["ss"]))
```

```python
import functools
import math

import numpy as np
import jax
import jax.numpy as jnp
from jax import lax
from jax.experimental import pallas as pl
from jax.experimental.pallas import tpu as pltpu

_F32 = jnp.float32
_BF16 = jnp.bfloat16

EPS = 1e-5
ROPE_THETA = 500000.0
HGRN_CHUNK = 64
V7X_VMEM_BYTES = 64 * 1024 * 1024
LANES = 128


def _compiler_params(semantics, block_bytes):
    limit = min(int(block_bytes * 1.25) + (8 << 20), V7X_VMEM_BYTES - (6 << 20))
    return pltpu.CompilerParams(dimension_semantics=semantics, vmem_limit_bytes=limit)


def _nbytes(shape, dtype):
    return int(np.prod(shape)) * jnp.dtype(dtype).itemsize


class _Casts:
    def __init__(self, casts, grid):
        steps_j = grid[1]
        self.views, self.in_specs, self.out_specs, self.out_shapes, self.shapes = [], [], [], [], []
        self.bytes = 0
        for w, layer in casts:
            depth, K, N = w.shape
            slabs = max(s for s in range(1, grid[0] * grid[1] + 1) if K % s == 0 and (K // s) % 16 == 0)
            r = K // slabs

            def index(i, j, layer=layer, slabs=slabs):
                return (layer, jnp.minimum(i * steps_j + j, slabs - 1), 0, 0)

            self.views.append(w.reshape(depth, slabs, r, N))
            self.in_specs.append(pl.BlockSpec((None, None, r, N), index))
            self.out_specs.append(pl.BlockSpec((None, r, N), lambda i, j, index=index: index(i, j)[1:]))
            self.out_shapes.append(jax.ShapeDtypeStruct((slabs, r, N), _BF16))
            self.shapes.append((K, N))
            self.bytes += 2 * (_nbytes((r, N), _F32) + _nbytes((r, N), _BF16))

    def __len__(self):
        return len(self.views)

    @staticmethod
    def run(in_refs, out_refs):
        for src, dst in zip(in_refs, out_refs):
            dst[...] = src[...].astype(_BF16)

    def finish(self, outs):
        return [o.reshape(shape) for o, shape in zip(outs, self.shapes)]


def _mm_body(x_ref, w_ref, *rest, n_casts):
    o_ref = rest[n_casts]
    o_ref[...] = jnp.dot(x_ref[...], w_ref[...], preferred_element_type=_F32).astype(o_ref.dtype)
    _Casts.run(rest[:n_casts], rest[n_casts + 1:])


def _matmul(x, w, *, tm, tn, casts=(), out_dtype=_F32, name):
    M, K = x.shape
    N = w.shape[1]
    assert M % tm == 0 and N % tn == 0
    grid = (M // tm, N // tn)
    side = _Casts(casts, grid)
    blocks = 2 * (_nbytes((tm, K), x.dtype) + _nbytes((K, tn), w.dtype) + _nbytes((tm, tn), out_dtype))
    outs = pl.pallas_call(
        functools.partial(_mm_body, n_casts=len(side)),
        grid=grid,
        in_specs=[pl.BlockSpec((tm, K), lambda i, j: (i, 0)),
                  pl.BlockSpec((K, tn), lambda i, j: (0, j))] + side.in_specs,
        out_specs=[pl.BlockSpec((tm, tn), lambda i, j: (i, j))] + side.out_specs,
        out_shape=[jax.ShapeDtypeStruct((M, N), out_dtype)] + side.out_shapes,
        compiler_params=_compiler_params(("arbitrary", "arbitrary"),
                                         blocks + side.bytes + _nbytes((tm, tn), _F32)),
        name=name,
    )(x, w, *side.views)
    return outs[0], side.finish(outs[1:])


def _mm2_body(xa_ref, xb_ref, wa_ref, wb_ref, o_ref):
    o_ref[...] = (jnp.dot(xa_ref[...], wa_ref[...], preferred_element_type=_F32)
                  + jnp.dot(xb_ref[...], wb_ref[...], preferred_element_type=_F32)).astype(o_ref.dtype)


def _matmul_concat(xa, xb, w, *, tm, tn, name):
    M, ka = xa.shape
    kb = xb.shape[1]
    N = w.shape[1]
    assert ka == kb and w.shape[0] == ka + kb and M % tm == 0 and N % tn == 0
    blocks = 2 * (2 * _nbytes((tm, ka), xa.dtype) + 2 * _nbytes((ka, tn), w.dtype) + _nbytes((tm, tn), _F32))
    return pl.pallas_call(
        _mm2_body,
        grid=(M // tm, N // tn),
        in_specs=[pl.BlockSpec((tm, ka), lambda i, j: (i, 0)),
                  pl.BlockSpec((tm, kb), lambda i, j: (i, 0)),
                  pl.BlockSpec((ka, tn), lambda i, j: (0, j)),
                  pl.BlockSpec((kb, tn), lambda i, j: (1, j))],
        out_specs=pl.BlockSpec((tm, tn), lambda i, j: (i, j)),
        out_shape=jax.ShapeDtypeStruct((M, N), _F32),
        compiler_params=_compiler_params(("parallel", "arbitrary"), blocks + 2 * _nbytes((tm, tn), _F32)),
        name=name,
    )(xa, xb, w, w)


def _ffn_up_body(x_ref, wg_ref, wu_ref, *rest, n_casts):
    o_ref = rest[n_casts]
    x = x_ref[...]
    g = jnp.dot(x, wg_ref[...], preferred_element_type=_F32)
    u = jnp.dot(x, wu_ref[...], preferred_element_type=_F32)
    o_ref[...] = (g / (1.0 + jnp.exp(-g)) * u).astype(o_ref.dtype)
    _Casts.run(rest[:n_casts], rest[n_casts + 1:])


def _ffn_up(x, wg, wu, *, tm, tn, casts=(), name):
    M, K = x.shape
    N = wg.shape[1]
    assert M % tm == 0 and N % tn == 0
    grid = (M // tm, N // tn)
    side = _Casts(casts, grid)
    blocks = 2 * (_nbytes((tm, K), x.dtype) + 2 * _nbytes((K, tn), wg.dtype) + _nbytes((tm, tn), _BF16))
    outs = pl.pallas_call(
        functools.partial(_ffn_up_body, n_casts=len(side)),
        grid=grid,
        in_specs=[pl.BlockSpec((tm, K), lambda i, j: (i, 0)),
                  pl.BlockSpec((K, tn), lambda i, j: (0, j)),
                  pl.BlockSpec((K, tn), lambda i, j: (0, j))] + side.in_specs,
        out_specs=[pl.BlockSpec((tm, tn), lambda i, j: (i, j))] + side.out_specs,
        out_shape=[jax.ShapeDtypeStruct((M, N), _BF16)] + side.out_shapes,
        compiler_params=_compiler_params(("arbitrary", "arbitrary"),
                                         blocks + side.bytes + 3 * _nbytes((tm, tn), _F32)),
        name=name,
    )(x, wg, wu, *side.views)
    return outs[0], side.finish(outs[1:])


def _ffn_up_cast_body(x_ref, wg_ref, wu_ref, o_ref, wgb_ref, wub_ref):
    x = x_ref[...]
    wg = wg_ref[...].astype(_BF16)
    wu = wu_ref[...].astype(_BF16)
    wgb_ref[...] = wg
    wub_ref[...] = wu
    g = jnp.dot(x, wg, preferred_element_type=_F32)
    u = jnp.dot(x, wu, preferred_element_type=_F32)
    o_ref[...] = (g / (1.0 + jnp.exp(-g)) * u).astype(o_ref.dtype)


def _ffn_up_cast(x, wg, wu, layer, *, tn, name):
    M, K = x.shape
    N = wg.shape[2]
    assert N % tn == 0
    blocks = 2 * (_nbytes((M, K), x.dtype) + 2 * _nbytes((K, tn), _F32) + 2 * _nbytes((K, tn), _BF16)
                  + _nbytes((M, tn), _BF16))
    w_f32 = pl.BlockSpec((None, K, tn), lambda j: (layer, 0, j))
    w_b16 = pl.BlockSpec((K, tn), lambda j: (0, j))
    return pl.pallas_call(
        _ffn_up_cast_body,
        grid=(N // tn,),
        in_specs=[pl.BlockSpec((M, K), lambda j: (0, 0)), w_f32, w_f32],
        out_specs=[pl.BlockSpec((M, tn), lambda j: (0, j)), w_b16, w_b16],
        out_shape=[jax.ShapeDtypeStruct((M, N), _BF16), jax.ShapeDtypeStruct((K, N), _BF16),
                   jax.ShapeDtypeStruct((K, N), _BF16)],
        compiler_params=_compiler_params(("parallel",), blocks + 2 * _nbytes((K, tn), _BF16)),
        name=name,
    )(x, wg, wu)


def _ln_body(x_ref, a_ref, g_ref, b_ref, y_ref, yb_ref, *, alpha, scale):
    z = alpha * x_ref[...] + scale * a_ref[...]
    mu = jnp.mean(z, axis=-1, keepdims=True)
    zc = z - mu
    var = jnp.mean(zc * zc, axis=-1, keepdims=True)
    y = zc * lax.rsqrt(var + EPS) * g_ref[...] + b_ref[...]
    y_ref[...] = y
    yb_ref[...] = y.astype(_BF16)


def _residual_ln(x, a, gain, bias, *, alpha, scale, tr, name):
    M, D = x.shape
    assert M % tr == 0
    blocks = 2 * (3 * _nbytes((tr, D), _F32) + _nbytes((tr, D), _BF16))
    row = pl.BlockSpec((tr, D), lambda i: (i, 0))
    vec = pl.BlockSpec((1, D), lambda i: (0, 0))
    return pl.pallas_call(
        functools.partial(_ln_body, alpha=alpha, scale=scale),
        grid=(M // tr,),
        in_specs=[row, row, vec, vec],
        out_specs=[row, row],
        out_shape=[jax.ShapeDtypeStruct((M, D), _F32), jax.ShapeDtypeStruct((M, D), _BF16)],
        compiler_params=_compiler_params(("parallel",), blocks + 2 * _nbytes((tr, D), _F32)),
        name=name,
    )(x, a, gain.reshape(1, D), bias.reshape(1, D))


def _rotary_tables(pos, dk, rot):
    half = rot // 2
    inv_freq = jnp.exp(-math.log(ROPE_THETA) * jnp.arange(half, dtype=_F32) / half)
    ang = pos.astype(_F32)[:, None] * inv_freq[None, :]
    cos, sin = jnp.cos(ang), jnp.sin(ang)
    n = pos.shape[0]
    ones = jnp.ones((n, dk - rot), _F32)
    zeros_h = jnp.zeros((n, half), _F32)
    zeros_r = jnp.zeros((n, dk - rot), _F32)
    c = jnp.concatenate([cos, cos, ones], axis=1)
    s_hi = jnp.concatenate([-sin, zeros_h, zeros_r], axis=1)
    s_lo = jnp.concatenate([zeros_h, sin, zeros_r], axis=1)
    return tuple(jnp.concatenate([t, t], axis=1) for t in (c, s_hi, s_lo))


def _rotary_body(q_ref, k_ref, c_ref, shi_ref, slo_ref, qo_ref, ko_ref, *, heads, width, half, q_scale):
    c, s_hi, s_lo = c_ref[...], shi_ref[...], slo_ref[...]
    for h in range(heads):
        cols = slice(h * width, (h + 1) * width)
        for src, dst, scale in ((q_ref, qo_ref, q_scale), (k_ref, ko_ref, None)):
            x = src[:, cols]
            y = x * c + pltpu.roll(x, width - half, 1) * s_hi + pltpu.roll(x, half, 1) * s_lo
            if scale is not None:
                y = y * scale
            dst[:, cols] = y.astype(dst.dtype)


def _rotary(proj, tables, *, heads, width, half, q_scale, tr, q_dtype, name):
    M = proj.shape[0]
    W = heads * width
    n_tab = tables[0].shape[0] // tr
    blocks = 2 * (2 * _nbytes((tr, W), _F32) + 3 * _nbytes((tr, width), _F32)
                  + _nbytes((tr, W), q_dtype) + _nbytes((tr, W), _F32))
    tab = pl.BlockSpec((tr, width), lambda i: (i % n_tab, 0))
    return pl.pallas_call(
        functools.partial(_rotary_body, heads=heads, width=width, half=half, q_scale=q_scale),
        grid=(M // tr,),
        in_specs=[pl.BlockSpec((tr, W), lambda i: (i, 0)), pl.BlockSpec((tr, W), lambda i: (i, 1)),
                  tab, tab, tab],
        out_specs=[pl.BlockSpec((tr, W), lambda i: (i, 0)), pl.BlockSpec((tr, W), lambda i: (i, 0))],
        out_shape=[jax.ShapeDtypeStruct((M, W), q_dtype), jax.ShapeDtypeStruct((M, W), _F32)],
        compiler_params=_compiler_params(("parallel",), blocks),
        name=name,
    )(proj, proj, *tables)


def _lambda_value(lq, lambda_init):
    s01 = jnp.sum(lq[0:1, :] * lq[1:2, :], axis=1, keepdims=True)
    s23 = jnp.sum(lq[2:3, :] * lq[3:4, :], axis=1, keepdims=True)
    return jnp.exp(s01) - jnp.exp(s23) + lambda_init


def _sub_rms(x, gain, lambda_init):
    ms = jnp.mean(x * x, axis=-1, keepdims=True)
    return x * lax.rsqrt(ms + EPS) * gain * (1.0 - lambda_init)


def _split_maps(q, dk):
    lane = lax.broadcasted_iota(jnp.int32, q.shape, 1)
    zero = jnp.zeros_like(q)
    return jnp.concatenate([jnp.where(lane < dk, q, zero), jnp.where(lane >= dk, q, zero)], axis=0)


def _prompt_attn_body(lq_ref, sg_ref, q_ref, k_ref, v_ref, o_ref, kb_ref, vt_ref, s_ref, p_ref, acc_ref, *,
                      tq, dk, lambda_init):
    qi = pl.program_id(2)
    n_blocks = k_ref.shape[0] // tq

    @pl.when(qi == 0)
    def _stage():
        for blk in range(n_blocks):
            rows = slice(blk * tq, (blk + 1) * tq)
            kb_ref[rows, :] = k_ref[rows, :].astype(_BF16)
            vt_ref[:, rows] = v_ref[rows, :].T.astype(_BF16)

    qt = q_ref[...].astype(_F32).T
    sub = lax.broadcasted_iota(jnp.int32, qt.shape, 0)
    qs = jnp.concatenate([jnp.where(sub < dk, qt, 0.0), jnp.where(sub >= dk, qt, 0.0)], axis=1).astype(_BF16)

    def scores(j):
        k = kb_ref[pl.ds(pl.multiple_of(j * tq, tq), tq), :]
        return jnp.dot(k, qs, preferred_element_type=_F32)

    def weighted_values(j, p):
        vt = vt_ref[:, pl.ds(pl.multiple_of(j * tq, tq), tq)]
        return jnp.dot(vt, p, preferred_element_type=_F32)

    def softmax_update(m, l, slot, masked):
        parts = []
        for c in range(0, 2 * tq, LANES):
            strip = slice(c, c + LANES)
            s = s_ref[slot, :, strip]
            if masked:
                key = lax.broadcasted_iota(jnp.int32, s.shape, 0)
                qry = lax.broadcasted_iota(jnp.int32, s.shape, 1) + (c % tq)
                s = jnp.where(key <= qry, s, -jnp.inf)
            m_new = jnp.maximum(m[:, strip], jnp.max(s, axis=0, keepdims=True))
            a = jnp.exp(m[:, strip] - m_new)
            p = jnp.exp(s - m_new)
            p_ref[slot, :, strip] = p.astype(_BF16)
            parts.append((m_new, a * l[:, strip] + jnp.sum(p, axis=0, keepdims=True), a))
        return tuple(jnp.concatenate(x, axis=1) for x in zip(*parts))

    def trip(j, carry):
        m, l, a_prev = carry
        cur = lax.rem(j, 2)
        pv = weighted_values(jnp.maximum(j - 1, 0), p_ref[1 - cur])
        m, l, a = softmax_update(m, l, cur, False)
        s_ref[1 - cur] = scores(j + 1)
        acc_ref[...] = a_prev * acc_ref[...] + pv
        return m, l, a

    s_ref[0] = scores(0)
    p_ref[1] = jnp.zeros(p_ref.shape[1:], _BF16)
    acc_ref[...] = jnp.zeros(acc_ref.shape, _F32)
    init = (jnp.full((1, 2 * tq), -jnp.inf, _F32), jnp.zeros((1, 2 * tq), _F32), jnp.ones((1, 2 * tq), _F32))
    m, l, a_prev = lax.fori_loop(0, qi, trip, init)
    cur = lax.rem(qi, 2)
    acc = a_prev * acc_ref[...] + weighted_values(jnp.maximum(qi - 1, 0), p_ref[1 - cur])
    m, l, a = softmax_update(m, l, cur, True)
    acc = a * acc + weighted_values(qi, p_ref[cur])
    o = acc / l
    lam = _lambda_value(lq_ref[...], lambda_init)
    out = o[:, :tq] - lam * o[:, tq:]
    ms = jnp.mean(out * out, axis=0, keepdims=True)
    out = out * lax.rsqrt(ms + EPS) * sg_ref[...] * (1.0 - lambda_init)
    o_ref[...] = out.T.astype(o_ref.dtype)


def _prompt_attention(q, k, proj, lq, subln, *, batch, seq, heads, dk, v_col0, lambda_init, tq, name):
    M = q.shape[0]
    dv = 2 * dk
    nq = seq // tq
    blocks = 2 * (_nbytes((tq, dv), _BF16) * 2 + 2 * _nbytes((seq, dv), _F32))
    scratch = 6 * _nbytes((2 * tq, tq), _F32)
    return pl.pallas_call(
        functools.partial(_prompt_attn_body, tq=tq, dk=dk, lambda_init=lambda_init),
        grid=(batch, heads, nq),
        in_specs=[pl.BlockSpec((4, dk), lambda b, h, i: (0, 0)),
                  pl.BlockSpec((dv, 1), lambda b, h, i: (0, 0)),
                  pl.BlockSpec((tq, dv), lambda b, h, i: (b * nq + i, h)),
                  pl.BlockSpec((seq, dv), lambda b, h, i: (b, h)),
                  pl.BlockSpec((seq, dv), lambda b, h, i: (b, v_col0 // dv + h))],
        out_specs=pl.BlockSpec((tq, dv), lambda b, h, i: (b * nq + i, h)),
        out_shape=jax.ShapeDtypeStruct((M, heads * dv), _BF16),
        scratch_shapes=[pltpu.VMEM((seq, dv), _BF16), pltpu.VMEM((dv, seq), _BF16),
                        pltpu.VMEM((2, tq, 2 * tq), _F32), pltpu.VMEM((2, tq, 2 * tq), _BF16),
                        pltpu.VMEM((dv, 2 * tq), _F32)],
        compiler_params=_compiler_params(("parallel", "parallel", "arbitrary"), blocks + scratch),
        name=name,
    )(lq, subln.reshape(dv, 1), q, k, proj)


def _decode_attn_body(pt_ref, lq_ref, sg_ref, q_ref, kn_ref, vn_ref, *rest, n_fetch, heads, dk, lambda_init):
    k_refs = rest[:n_fetch]
    v_refs = rest[n_fetch:2 * n_fetch]
    o_ref = rest[2 * n_fetch]
    qm_ref, bias_ref, m_ref, l_ref, acc_ref = rest[2 * n_fetch + 1:]
    j = pl.program_id(1)

    @pl.when(j == 0)
    def _init():
        qm_ref[...] = _split_maps(q_ref[...], dk).astype(_BF16)
        row = lax.broadcasted_iota(jnp.int32, bias_ref.shape, 0)
        col = lax.broadcasted_iota(jnp.int32, bias_ref.shape, 1)
        bias_ref[...] = jnp.where((row & (heads - 1)) == (col & (heads - 1)), 0.0, -jnp.inf)
        m_ref[...] = jnp.full(m_ref.shape, -jnp.inf, _F32)
        l_ref[...] = jnp.zeros(l_ref.shape, _F32)
        acc_ref[...] = jnp.zeros(acc_ref.shape, _F32)

    qm = qm_ref[...]
    for k_ref, v_ref in zip(k_refs, v_refs):
        k = k_ref[...].astype(_BF16)
        v = v_ref[...].astype(_BF16)
        s = lax.dot_general(qm, k, (((1,), (1,)), ((), ())), preferred_element_type=_F32) + bias_ref[...]
        m = m_ref[...]
        m_new = jnp.maximum(m, jnp.max(s, axis=1, keepdims=True))
        a = jnp.exp(m - m_new)
        p = jnp.exp(s - m_new)
        l_ref[...] = a * l_ref[...] + jnp.sum(p, axis=1, keepdims=True)
        acc_ref[...] = a * acc_ref[...] + jnp.dot(p.astype(_BF16), v, preferred_element_type=_F32)
        m_ref[...] = m_new

    @pl.when(j == pl.num_programs(1) - 1)
    def _finish():
        prod = _split_maps(q_ref[...] * kn_ref[...], dk)
        s_self = jnp.sum(prod, axis=1, keepdims=True)
        v_self = jnp.concatenate([vn_ref[...], vn_ref[...]], axis=0)
        m = m_ref[...]
        m_new = jnp.maximum(m, s_self)
        a = jnp.exp(m - m_new)
        p = jnp.exp(s_self - m_new)
        l = a * l_ref[...] + p
        o = (a * acc_ref[...] + p * v_self) / l
        lam = _lambda_value(lq_ref[...], lambda_init)
        out = o[:heads] - lam * o[heads:]
        o_ref[...] = _sub_rms(out, sg_ref[...], lambda_init)


def _decode_attention(q, k_new, v_new, cache_k, cache_v, layer, page_table, lq, subln, *, dk, lambda_init,
                      n_fetch, name):
    B, heads, dv = q.shape
    depth, n_pool, page = cache_k.shape[:3]
    n_pages = page_table.shape[1]
    assert n_pages % n_fetch == 0 and heads & (heads - 1) == 0
    rows = page * heads
    ck = cache_k.reshape(depth, n_pool, rows, dv)
    cv = cache_v.reshape(depth, n_pool, rows, dv)

    def page_spec(r):
        return pl.BlockSpec((None, None, rows, dv),
                            lambda b, j, pt: (layer, pt[b * n_pages + j * n_fetch + r], 0, 0))

    head_spec = pl.BlockSpec((None, heads, dv), lambda b, j, pt: (b, 0, 0))
    blocks = 2 * (2 * n_fetch * _nbytes((rows, dv), _F32) + 4 * _nbytes((heads, dv), _F32))
    scratch = 4 * _nbytes((2 * heads, rows), _F32) + 2 * _nbytes((rows, dv), _BF16)
    grid_spec = pltpu.PrefetchScalarGridSpec(
        num_scalar_prefetch=1,
        grid=(B, n_pages // n_fetch),
        in_specs=[pl.BlockSpec((4, dk), lambda b, j, pt: (0, 0)),
                  pl.BlockSpec((1, dv), lambda b, j, pt: (0, 0)),
                  head_spec, head_spec, head_spec]
                 + [page_spec(r) for r in range(n_fetch)] + [page_spec(r) for r in range(n_fetch)],
        out_specs=head_spec,
        scratch_shapes=[pltpu.VMEM((2 * heads, dv), _BF16), pltpu.VMEM((2 * heads, rows), _F32),
                        pltpu.VMEM((2 * heads, 1), _F32), pltpu.VMEM((2 * heads, 1), _F32),
                        pltpu.VMEM((2 * heads, dv), _F32)],
    )
    return pl.pallas_call(
        functools.partial(_decode_attn_body, n_fetch=n_fetch, heads=heads, dk=dk, lambda_init=lambda_init),
        grid_spec=grid_spec,
        out_shape=jax.ShapeDtypeStruct((B, heads, dv), _F32),
        compiler_params=_compiler_params(("parallel", "arbitrary"), blocks + scratch),
        name=name,
    )(page_table.reshape(-1), lq, subln.reshape(1, dv), q, k_new, v_new,
      *([ck] * n_fetch), *([cv] * n_fetch))


def _hgrn_constants(c):
    t = np.arange(c)
    tt, uu = t[:, None], t[None, :]
    masks = []
    s = c // 2
    while s >= 1:
        x = tt ^ uu
        wide = np.zeros((c, 2 * c), np.float32)
        off = (len(masks) % 2) * c
        wide[:, off:off + c] = (x >= s) & (x < 2 * s) & (tt > uu)
        masks.append(wide)
        s //= 2
    return jnp.asarray(np.stack(masks))


def _level_reference(b, s):
    n = b.shape[0]
    if 2 * s >= 8:
        pieces = [jnp.broadcast_to(b[base + s - 1:base + s, :], (2 * s, b.shape[1])) for base in range(0, n, 2 * s)]
        return jnp.concatenate(pieces, axis=0)
    pos = lax.broadcasted_iota(jnp.int32, b.shape, 0) & (2 * s - 1)
    ref = b
    for p in range(2 * s):
        shift = p - (s - 1)
        if shift != 0:
            ref = jnp.where(pos == p, pltpu.roll(b, shift % n, 0), ref)
    return ref


def _sigmoid(x):
    return 1.0 / (1.0 + jnp.exp(-x))


def _lower_bound(rows, layer):
    top = functools.reduce(jnp.maximum, rows)
    e = [jnp.exp(r - top) for r in rows]
    total = functools.reduce(lambda a, b: a + b, e)
    lb = jnp.zeros_like(total)
    for j in range(1, layer + 1):
        lb = lb + e[j] / total
    return lb


def _hgrn_prompt_body(mask_ref, lbraw_ref, gn_ref, qh_ref, fh_ref, ih_ref, gh_ref, o_ref, s_ref, st_ref, *,
                      layer, chunk, n_chunks):
    n_levels = mask_ref.shape[0]
    width = LANES
    n_heads = st_ref.shape[0]
    lb_all = _lower_bound([lbraw_ref[j:j + 1, :] for j in range(lbraw_ref.shape[0])], layer)
    gn = gn_ref[...]
    row = lax.broadcasted_iota(jnp.int32, (chunk, width), 0)
    st_ref[...] = jnp.zeros(st_ref.shape, _F32)

    def head_chunk(ci, hh):
        rows = pl.ds(pl.multiple_of(ci * chunk, chunk), chunk)
        cols = slice(hh * width, (hh + 1) * width)
        lb = lb_all[:, cols]
        qh, fh, v, gh = qh_ref[rows, cols], fh_ref[rows, cols], ih_ref[rows, cols], gh_ref[rows, cols]
        fg = lb + (1.0 - lb) * _sigmoid(fh)
        q = qh * _sigmoid(qh)
        k = 1.0 - fg
        b = jnp.log(fg)
        d = 1
        while d < chunk:
            b = b + jnp.where(row >= d, pltpu.roll(b, d, 0), 0.0)
            d *= 2
        b_last = b[chunk - 1:chunk]

        def level_operands(s):
            upper = (row & s) != 0
            gap = b - _level_reference(b, s)
            dec = jnp.exp(jnp.where(upper, gap, -gap))
            return jnp.where(upper, q * dec, 0.0), jnp.where(upper, 0.0, k * dec)

        attn = jnp.zeros((chunk, 2 * chunk), _F32)
        for lvl in range(0, n_levels, 2):
            qa, ka = level_operands(chunk >> (lvl + 1))
            qb, kb = level_operands(chunk >> (lvl + 2))
            r = lax.dot_general(jnp.concatenate([qa, qb], axis=0).astype(_BF16),
                                jnp.concatenate([ka, kb], axis=0).astype(_BF16),
                                (((1,), (1,)), ((), ())), preferred_element_type=_F32)
            attn = attn + r[:chunk] * mask_ref[lvl] + r[chunk:] * mask_ref[lvl + 1]
        vt2 = jnp.concatenate([v, v], axis=0).T.astype(_BF16)
        st = st_ref[hh]
        lhs = jnp.concatenate([(q * jnp.exp(b)).astype(_BF16), attn.astype(_BF16)], axis=1)
        rhs = jnp.concatenate([st.astype(_BF16), vt2], axis=1)
        o = lax.dot_general(lhs, rhs, (((1,), (1,)), ((), ())), preferred_element_type=_F32)
        o = o + jnp.sum(q * k, axis=1, keepdims=True) * v
        k_dec = (k * jnp.exp(b_last - b)).astype(_BF16)
        st_ref[hh] = st * jnp.exp(b_last) + jnp.dot(vt2[:, :chunk], k_dec, preferred_element_type=_F32)
        ms = jnp.mean(o * o, axis=-1, keepdims=True)
        o_ref[rows, cols] = (o * lax.rsqrt(ms + EPS) * gn * (gh * _sigmoid(gh))).astype(o_ref.dtype)

    def body(ci, _):
        for hh in range(n_heads):
            head_chunk(ci, hh)
        return 0

    lax.fori_loop(0, n_chunks, body, 0)
    for hh in range(n_heads):
        s_ref[hh] = st_ref[hh].T


def _hgrn_prompt(proj, lb_raw, gnorm, layer, *, batch, seq, heads, col0, heads_per_step, name):
    M = proj.shape[0]
    depth, n_lb = lb_raw.shape
    F = n_lb // heads
    I = gnorm.shape[0]
    hp = heads_per_step
    assert F == LANES and I == LANES and heads % hp == 0 and col0 % (hp * LANES) == 0
    chunk = math.gcd(seq, HGRN_CHUNK)
    masks = _hgrn_constants(chunk)
    assert masks.shape[0] % 2 == 0 and 2 * chunk == LANES
    wide = hp * LANES

    def col(group):
        return pl.BlockSpec((seq, wide), lambda b, h: (b, (col0 + group * heads * LANES) // wide + h))

    blocks = 2 * (4 * _nbytes((seq, wide), _F32) + _nbytes((seq, wide), _BF16)
                  + _nbytes(masks.shape, _F32) + _nbytes((hp, F, I), _F32))
    return pl.pallas_call(
        functools.partial(_hgrn_prompt_body, layer=layer, chunk=chunk, n_chunks=seq // chunk),
        grid=(batch, heads // hp),
        in_specs=[pl.BlockSpec(masks.shape, lambda b, h: (0, 0, 0)),
                  pl.BlockSpec((depth, wide), lambda b, h: (0, h)),
                  pl.BlockSpec((1, I), lambda b, h: (0, 0)),
                  col(0), col(1), col(2), col(3)],
        out_specs=[pl.BlockSpec((seq, wide), lambda b, h: (b, h)),
                   pl.BlockSpec((None, hp, F, I), lambda b, h: (b, h, 0, 0))],
        out_shape=[jax.ShapeDtypeStruct((M, heads * I), _BF16),
                   jax.ShapeDtypeStruct((batch, heads, F, I), _F32)],
        scratch_shapes=[pltpu.VMEM((hp, I, F), _F32)],
        compiler_params=_compiler_params(("parallel", "parallel"), blocks + (4 << 20)),
        name=name,
    )(masks, lb_raw, gnorm.reshape(1, I), proj, proj, proj, proj)


def _hgrn_step_body(lbraw_ref, gn_ref, qh_ref, fh_ref, ih_ref, gh_ref, s0_ref, o_ref, s_ref, *, layer, heads):
    F = fh_ref.shape[1]
    lb = _lower_bound([lbraw_ref[j] for j in range(lbraw_ref.shape[0])], layer)
    fg = lb + (1.0 - lb) * _sigmoid(fh_ref[...])
    qh = qh_ref[...]
    q = qh * _sigmoid(qh)
    k = 1.0 - fg
    pad = jnp.zeros((F - 3 * heads, F), _F32)
    cols = jnp.concatenate([fg, k, q, pad], axis=0).T
    v = ih_ref[...]
    outs = []
    for h in range(heads):
        s_new = cols[:, h:h + 1] * s0_ref[h] + cols[:, heads + h:heads + h + 1] * v[h:h + 1, :]
        s_ref[h] = s_new
        outs.append(jnp.sum(cols[:, 2 * heads + h:2 * heads + h + 1] * s_new, axis=0, keepdims=True))
    o = jnp.concatenate(outs, axis=0)
    gh = gh_ref[...]
    ms = jnp.mean(o * o, axis=-1, keepdims=True)
    o_ref[...] = o * lax.rsqrt(ms + EPS) * gn_ref[...] * (gh * _sigmoid(gh))


def _hgrn_step(proj3, lb_raw, gnorm, state, layer, *, heads, group0, name):
    B = proj3.shape[0]
    depth, n_lb = lb_raw.shape
    F, I = state.shape[3], state.shape[4]
    assert F == LANES and I == LANES and 3 * heads <= F

    def grp(g):
        return pl.BlockSpec((None, heads, LANES), lambda b: (b, group0 + g, 0))

    blocks = 2 * (2 * _nbytes((heads, F, I), _F32) + 6 * _nbytes((heads, LANES), _F32))
    return pl.pallas_call(
        functools.partial(_hgrn_step_body, layer=layer, heads=heads),
        grid=(B,),
        in_specs=[pl.BlockSpec((depth, heads, F), lambda b: (0, 0, 0)),
                  pl.BlockSpec((1, I), lambda b: (0, 0)),
                  grp(0), grp(1), grp(2), grp(3),
                  pl.BlockSpec((None, None, heads, F, I), lambda b: (layer, b, 0, 0, 0))],
        out_specs=[pl.BlockSpec((None, heads, I), lambda b: (b, 0, 0)),
                   pl.BlockSpec((None, heads, F, I), lambda b: (b, 0, 0, 0))],
        out_shape=[jax.ShapeDtypeStruct((B, heads, I), _F32),
                   jax.ShapeDtypeStruct((B, heads, F, I), _F32)],
        compiler_params=_compiler_params(("parallel",), blocks + (2 << 20)),
        name=name,
    )(lb_raw.reshape(depth, heads, F), gnorm.reshape(1, I), proj3, proj3, proj3, proj3, state)


def _pick(n, candidates):
    for c in candidates:
        if n % c == 0:
            return c
    return n


def _tiles(M, d_model, d_ff, in_cols):
    tm = _pick(M, (1024, 512, 256, 128, 64, 32, 16))
    return dict(
        tm=tm,
        tm_up=_pick(M, (2048, 1024, 512, 256, 128, 64, 32, 16)),
        tn_up=_pick(d_ff, (256, 128)),
        tm_down=min(tm, 512),
        tn_down=_pick(d_model, (512, 256, 128)),
        tn_in=_pick(in_cols, (512, 256, 128)),
        tn_out=_pick(d_model, (1024, 512, 256, 128)),
        tr=_pick(M, (256, 128, 64, 32, 16)),
    )


def _decode_tiles(m_s, d_model, d_ff, in_cols):
    return dict(tm=m_s, tn_up=_pick(d_ff, (256, 128)), tn_down=_pick(d_model, (512, 256, 128)),
                tn_in=_pick(in_cols, (1024, 512, 256, 128)), tn_out=_pick(d_model, (1024, 512, 256, 128)))


def _ffn_blocks(xp, xpb, xs, xsb, wg_b, wu_b, wd_cast, gain, bias, alpha, t_p, t_s, tag, *, h_s=None,
                up_casts=(), down_casts=()):
    h_p, cast = _ffn_up(xpb, wg_b, wu_b, tm=t_p["tm_up"], tn=t_p["tn_up"], casts=[wd_cast, *up_casts],
                        name=f"ffn_up_p{tag}")
    wd_b = cast[0]
    if h_s is None:
        h_s, _ = _ffn_up(xsb, wg_b, wu_b, tm=t_s["tm"], tn=t_s["tn_up"], name=f"ffn_up_s{tag}")
    a_s, _ = _matmul(h_s, wd_b, tm=t_s["tm"], tn=t_s["tn_down"], name=f"ffn_down_s{tag}")
    xs, xsb = _residual_ln(xs, a_s, gain, bias, alpha=alpha, scale=0.5, tr=t_s["tm"], name=f"ln_s{tag}")
    a_p, down_cast = _matmul(h_p, wd_b, tm=t_p["tm_down"], tn=t_p["tn_down"], casts=down_casts,
                             name=f"ffn_down_p{tag}")
    xp, xpb = _residual_ln(xp, a_p, gain, bias, alpha=alpha, scale=0.5, tr=t_p["tr"], name=f"ln_p{tag}")
    return xp, xpb, xs, xsb, cast[1:], down_cast


def kernel(x_prompt, x_sample, cache_k, cache_v, state_hgrn, page_table, ln_gain, ln_bias, ffn1_gate, ffn1_up, ffn1_down, w_in, lambda_qk, subln_gain, hgrn_lower_bound, hgrn_gnorm_gain, w_out, ffn2_gate, ffn2_up, ffn2_down):
    batch, seq, d_model = x_prompt.shape
    dec_batch, dec_seq, _ = x_sample.shape
    assert dec_seq == 1
    depth = w_in.shape[0]
    att_heads, dv = cache_v.shape[3], cache_v.shape[4]
    dk = dv // 2
    assert cache_k.shape[4] == dv == LANES
    rot = dk // 4
    hg_heads, hg_f, hg_i = state_hgrn.shape[2:]
    d_ff = ffn1_gate.shape[2]
    in_cols = w_in.shape[2]
    att_width = att_heads * dv
    hg_col0 = 3 * att_width
    alpha = (2 * depth) ** 0.25
    att_scale = dk ** -0.5
    n_past = page_table.shape[1] * cache_k.shape[2]

    m_p = batch * seq
    m_s = 16
    t_p = _tiles(m_p, d_model, d_ff, in_cols)
    t_s = _decode_tiles(m_s, d_model, d_ff, in_cols)
    tab_p = _rotary_tables(jnp.arange(seq, dtype=jnp.int32), dk, rot)
    tab_s = _rotary_tables(jnp.full((m_s,), n_past, jnp.int32), dk, rot)
    tq = _pick(seq, (256, 128))

    xp = x_prompt.reshape(m_p, d_model)
    xs = jnp.pad(x_sample.reshape(dec_batch, d_model), ((0, m_s - dec_batch), (0, 0)))
    xpb, xsb = xp.astype(_BF16), xs.astype(_BF16)

    h_s0, wg_b, wu_b = _ffn_up_cast(xsb, ffn1_gate, ffn1_up, 0, tn=t_s["tn_up"], name="ffn_up_s0a")

    outs = {k: [] for k in ("kp", "vp", "sp", "ks", "vs", "ss")}
    for l in range(depth):
        lambda_init = 0.8 - 0.6 * math.exp(-0.3 * l)
        gains, biases = ln_gain[l], ln_bias[l]

        xp, xpb, xs, xsb, (win_b,), (wout_b,) = _ffn_blocks(
            xp, xpb, xs, xsb, wg_b, wu_b, (ffn1_down, l), gains[0], biases[0], alpha, t_p, t_s, f"{l}a",
            h_s=h_s0 if l == 0 else None, up_casts=[(w_in, l)], down_casts=[(w_out, l)])

        proj_s, _ = _matmul(xsb, win_b, tm=t_s["tm"], tn=t_s["tn_in"], name=f"w_in_s{l}")
        q_s, k_s = _rotary(proj_s, tab_s, heads=att_heads, width=dv, half=rot // 2, q_scale=att_scale,
                           tr=m_s, q_dtype=_F32, name=f"rotary_s{l}")
        proj3 = proj_s[:dec_batch].reshape(dec_batch, in_cols // LANES, LANES)
        q3 = q_s[:dec_batch].reshape(dec_batch, att_heads, dv)
        k3 = k_s[:dec_batch].reshape(dec_batch, att_heads, dv)
        v3 = proj_s[:dec_batch, 2 * att_width:3 * att_width].reshape(dec_batch, att_heads, dv)
        att_s = _decode_attention(q3, k3, v3, cache_k, cache_v, l, page_table, lambda_qk[l], subln_gain[l],
                                  dk=dk, lambda_init=lambda_init, n_fetch=_pick(page_table.shape[1], (8, 4, 2, 1)),
                                  name=f"attn_s{l}")
        o_s, st_s = _hgrn_step(proj3, hgrn_lower_bound, hgrn_gnorm_gain[l], state_hgrn, l, heads=hg_heads,
                               group0=hg_col0 // (hg_heads * LANES), name=f"hgrn_s{l}")
        merged_s = jnp.concatenate([att_s.reshape(dec_batch, att_width), o_s.reshape(dec_batch, hg_heads * hg_i)],
                                   axis=1)
        merged_s = jnp.pad(merged_s, ((0, m_s - dec_batch), (0, 0))).astype(_BF16)
        mixed_s, _ = _matmul(merged_s, wout_b, tm=t_s["tm"], tn=t_s["tn_out"], name=f"w_out_s{l}")
        xs, xsb = _residual_ln(xs, mixed_s, gains[1], biases[1], alpha=alpha, scale=1.0, tr=m_s,
                               name=f"ln_s{l}b")
        outs["ks"].append(k3.reshape(dec_batch, 1, att_heads, dv))
        outs["vs"].append(v3.reshape(dec_batch, 1, att_heads, dv))
        outs["ss"].append(st_s)

        proj, (wg_c, wu_c) = _matmul(xpb, win_b, tm=t_p["tm"], tn=t_p["tn_in"],
                                     casts=[(ffn2_gate, l), (ffn2_up, l)], name=f"w_in_p{l}")
        q_rot, k_rot = _rotary(proj, tab_p, heads=att_heads, width=dv, half=rot // 2, q_scale=att_scale,
                               tr=min(t_p["tr"], seq), q_dtype=_BF16, name=f"rotary_p{l}")
        att = _prompt_attention(q_rot, k_rot, proj, lambda_qk[l], subln_gain[l], batch=batch, seq=seq,
                                heads=att_heads, dk=dk, v_col0=2 * att_width, lambda_init=lambda_init, tq=tq,
                                name=f"attn_p{l}")
        o_h, s_new = _hgrn_prompt(proj, hgrn_lower_bound, hgrn_gnorm_gain[l], l, batch=batch, seq=seq,
                                  heads=hg_heads, col0=hg_col0, heads_per_step=_pick(hg_heads, (4, 2, 1)),
                                  name=f"hgrn_p{l}")
        mixed = _matmul_concat(att, o_h, wout_b, tm=t_p["tm"], tn=t_p["tn_out"], name=f"w_out_p{l}")
        xp, xpb = _residual_ln(xp, mixed, gains[1], biases[1], alpha=alpha, scale=1.0, tr=t_p["tr"],
                               name=f"ln_p{l}b")
        outs["kp"].append(k_rot.reshape(batch, seq, att_heads, dv))
        outs["vp"].append(proj[:, 2 * att_width:3 * att_width].reshape(batch, seq, att_heads, dv))
        outs["sp"].append(s_new)

        ahead = [(ffn1_gate, l + 1), (ffn1_up, l + 1)] if l + 1 < depth else []
        xp, xpb, xs, xsb, nxt, _ = _ffn_blocks(
            xp, xpb, xs, xsb, wg_c, wu_c, (ffn2_down, l), gains[2], biases[2], alpha, t_p, t_s, f"{l}c",
            up_casts=ahead)
        if ahead:
            wg_b, wu_b = nxt

    return (xp.reshape(batch, seq, d_model), xs[:dec_batch].reshape(dec_batch, 1, d_model),
            jnp.stack(outs["kp"]), jnp.stack(outs["vp"]), jnp.stack(outs["sp"]),
            jnp.stack(outs["ks"]), jnp.stack(outs["vs"]), jnp.stack(outs["ss"]))
```

```python
import functools
import math

import numpy as np
import jax
import jax.numpy as jnp
from jax import lax
from jax.experimental import pallas as pl
from jax.experimental.pallas import tpu as pltpu

_F32 = jnp.float32
_BF16 = jnp.bfloat16

EPS = 1e-5
ROPE_THETA = 500000.0
HGRN_CHUNK = 64
V7X_VMEM_BYTES = 64 * 1024 * 1024
LANES = 128


def _compiler_params(semantics, block_bytes):
    limit = min(int(block_bytes * 1.25) + (8 << 20), V7X_VMEM_BYTES - (6 << 20))
    return pltpu.CompilerParams(dimension_semantics=semantics, vmem_limit_bytes=limit)


def _nbytes(shape, dtype):
    return int(np.prod(shape)) * jnp.dtype(dtype).itemsize


class _Casts:
    def __init__(self, casts, grid):
        strides = [int(np.prod(grid[d + 1:])) for d in range(len(grid))]
        steps = int(np.prod(grid))
        self.views, self.in_specs, self.out_specs, self.out_shapes, self.shapes = [], [], [], [], []
        self.bytes = 0
        for w, layer in casts:
            depth, K, N = w.shape
            slabs = max(s for s in range(1, steps + 1) if K % s == 0 and (K // s) % 16 == 0)
            r = K // slabs

            def index(*ids, layer=layer, slabs=slabs):
                step = sum(i * s for i, s in zip(ids, strides))
                return (layer, jnp.minimum(step, slabs - 1), 0, 0)

            self.views.append(w.reshape(depth, slabs, r, N))
            self.in_specs.append(pl.BlockSpec((None, None, r, N), index))
            self.out_specs.append(pl.BlockSpec((None, r, N), lambda *ids, index=index: index(*ids)[1:]))
            self.out_shapes.append(jax.ShapeDtypeStruct((slabs, r, N), _BF16))
            self.shapes.append((K, N))
            self.bytes += 2 * (_nbytes((r, N), _F32) + _nbytes((r, N), _BF16))

    def __len__(self):
        return len(self.views)

    @staticmethod
    def run(in_refs, out_refs):
        for src, dst in zip(in_refs, out_refs):
            dst[...] = src[...].astype(_BF16)

    def finish(self, outs):
        return [o.reshape(shape) for o, shape in zip(outs, self.shapes)]


def _mm_body(x_ref, w_ref, *rest, n_casts):
    o_ref = rest[n_casts]
    o_ref[...] = jnp.dot(x_ref[...], w_ref[...], preferred_element_type=_F32).astype(o_ref.dtype)
    _Casts.run(rest[:n_casts], rest[n_casts + 1:])


def _matmul(x, w, *, tm, tn, casts=(), out_dtype=_F32, name):
    M, K = x.shape
    N = w.shape[1]
    assert M % tm == 0 and N % tn == 0
    grid = (M // tm, N // tn)
    side = _Casts(casts, grid)
    blocks = 2 * (_nbytes((tm, K), x.dtype) + _nbytes((K, tn), w.dtype) + _nbytes((tm, tn), out_dtype))
    outs = pl.pallas_call(
        functools.partial(_mm_body, n_casts=len(side)),
        grid=grid,
        in_specs=[pl.BlockSpec((tm, K), lambda i, j: (i, 0)),
                  pl.BlockSpec((K, tn), lambda i, j: (0, j))] + side.in_specs,
        out_specs=[pl.BlockSpec((tm, tn), lambda i, j: (i, j))] + side.out_specs,
        out_shape=[jax.ShapeDtypeStruct((M, N), out_dtype)] + side.out_shapes,
        compiler_params=_compiler_params(("arbitrary", "arbitrary"),
                                         blocks + side.bytes + _nbytes((tm, tn), _F32)),
        name=name,
    )(x, w, *side.views)
    return outs[0], side.finish(outs[1:])


def _mm2_body(xa_ref, xb_ref, wa_ref, wb_ref, o_ref):
    o_ref[...] = (jnp.dot(xa_ref[...], wa_ref[...], preferred_element_type=_F32)
                  + jnp.dot(xb_ref[...], wb_ref[...], preferred_element_type=_F32)).astype(o_ref.dtype)


def _matmul_concat(xa, xb, w, *, tm, tn, name):
    M, ka = xa.shape
    kb = xb.shape[1]
    N = w.shape[1]
    assert ka == kb and w.shape[0] == ka + kb and M % tm == 0 and N % tn == 0
    blocks = 2 * (2 * _nbytes((tm, ka), xa.dtype) + 2 * _nbytes((ka, tn), w.dtype) + _nbytes((tm, tn), _F32))
    return pl.pallas_call(
        _mm2_body,
        grid=(M // tm, N // tn),
        in_specs=[pl.BlockSpec((tm, ka), lambda i, j: (i, 0)),
                  pl.BlockSpec((tm, kb), lambda i, j: (i, 0)),
                  pl.BlockSpec((ka, tn), lambda i, j: (0, j)),
                  pl.BlockSpec((kb, tn), lambda i, j: (1, j))],
        out_specs=pl.BlockSpec((tm, tn), lambda i, j: (i, j)),
        out_shape=jax.ShapeDtypeStruct((M, N), _F32),
        compiler_params=_compiler_params(("parallel", "arbitrary"), blocks + 2 * _nbytes((tm, tn), _F32)),
        name=name,
    )(xa, xb, w, w)


def _ffn_up_body(x_ref, wg_ref, wu_ref, *rest, n_casts):
    o_ref = rest[n_casts]
    x = x_ref[...]
    g = jnp.dot(x, wg_ref[...], preferred_element_type=_F32)
    u = jnp.dot(x, wu_ref[...], preferred_element_type=_F32)
    o_ref[...] = (g / (1.0 + jnp.exp(-g)) * u).astype(o_ref.dtype)
    _Casts.run(rest[:n_casts], rest[n_casts + 1:])


def _ffn_up(x, wg, wu, *, tm, tn, casts=(), name):
    M, K = x.shape
    N = wg.shape[1]
    assert M % tm == 0 and N % tn == 0
    grid = (M // tm, N // tn)
    side = _Casts(casts, grid)
    blocks = 2 * (_nbytes((tm, K), x.dtype) + 2 * _nbytes((K, tn), wg.dtype) + _nbytes((tm, tn), _BF16))
    outs = pl.pallas_call(
        functools.partial(_ffn_up_body, n_casts=len(side)),
        grid=grid,
        in_specs=[pl.BlockSpec((tm, K), lambda i, j: (i, 0)),
                  pl.BlockSpec((K, tn), lambda i, j: (0, j)),
                  pl.BlockSpec((K, tn), lambda i, j: (0, j))] + side.in_specs,
        out_specs=[pl.BlockSpec((tm, tn), lambda i, j: (i, j))] + side.out_specs,
        out_shape=[jax.ShapeDtypeStruct((M, N), _BF16)] + side.out_shapes,
        compiler_params=_compiler_params(("arbitrary", "arbitrary"),
                                         blocks + side.bytes + 3 * _nbytes((tm, tn), _F32)),
        name=name,
    )(x, wg, wu, *side.views)
    return outs[0], side.finish(outs[1:])


def _ffn_up_cast_body(x_ref, wg_ref, wu_ref, o_ref, wgb_ref, wub_ref):
    x = x_ref[...]
    wg = wg_ref[...].astype(_BF16)
    wu = wu_ref[...].astype(_BF16)
    wgb_ref[...] = wg
    wub_ref[...] = wu
    g = jnp.dot(x, wg, preferred_element_type=_F32)
    u = jnp.dot(x, wu, preferred_element_type=_F32)
    o_ref[...] = (g / (1.0 + jnp.exp(-g)) * u).astype(o_ref.dtype)


def _ffn_up_cast(x, wg, wu, layer, *, tn, name):
    M, K = x.shape
    N = wg.shape[2]
    assert N % tn == 0
    blocks = 2 * (_nbytes((M, K), x.dtype) + 2 * _nbytes((K, tn), _F32) + 2 * _nbytes((K, tn), _BF16)
                  + _nbytes((M, tn), _BF16))
    w_f32 = pl.BlockSpec((None, K, tn), lambda j: (layer, 0, j))
    w_b16 = pl.BlockSpec((K, tn), lambda j: (0, j))
    return pl.pallas_call(
        _ffn_up_cast_body,
        grid=(N // tn,),
        in_specs=[pl.BlockSpec((M, K), lambda j: (0, 0)), w_f32, w_f32],
        out_specs=[pl.BlockSpec((M, tn), lambda j: (0, j)), w_b16, w_b16],
        out_shape=[jax.ShapeDtypeStruct((M, N), _BF16), jax.ShapeDtypeStruct((K, N), _BF16),
                   jax.ShapeDtypeStruct((K, N), _BF16)],
        compiler_params=_compiler_params(("parallel",), blocks + 2 * _nbytes((K, tn), _BF16)),
        name=name,
    )(x, wg, wu)


def _ln_body(x_ref, a_ref, g_ref, b_ref, y_ref, yb_ref, *, alpha, scale):
    z = alpha * x_ref[...] + scale * a_ref[...]
    mu = jnp.mean(z, axis=-1, keepdims=True)
    zc = z - mu
    var = jnp.mean(zc * zc, axis=-1, keepdims=True)
    y = zc * lax.rsqrt(var + EPS) * g_ref[...] + b_ref[...]
    y_ref[...] = y
    yb_ref[...] = y.astype(_BF16)


def _residual_ln(x, a, gain, bias, *, alpha, scale, tr, name):
    M, D = x.shape
    assert M % tr == 0
    blocks = 2 * (3 * _nbytes((tr, D), _F32) + _nbytes((tr, D), _BF16))
    row = pl.BlockSpec((tr, D), lambda i: (i, 0))
    vec = pl.BlockSpec((1, D), lambda i: (0, 0))
    return pl.pallas_call(
        functools.partial(_ln_body, alpha=alpha, scale=scale),
        grid=(M // tr,),
        in_specs=[row, row, vec, vec],
        out_specs=[row, row],
        out_shape=[jax.ShapeDtypeStruct((M, D), _F32), jax.ShapeDtypeStruct((M, D), _BF16)],
        compiler_params=_compiler_params(("parallel",), blocks + 2 * _nbytes((tr, D), _F32)),
        name=name,
    )(x, a, gain.reshape(1, D), bias.reshape(1, D))


def _rotary_tables(pos, dk, rot):
    half = rot // 2
    inv_freq = jnp.exp(-math.log(ROPE_THETA) * jnp.arange(half, dtype=_F32) / half)
    ang = pos.astype(_F32)[:, None] * inv_freq[None, :]
    cos, sin = jnp.cos(ang), jnp.sin(ang)
    n = pos.shape[0]
    ones = jnp.ones((n, dk - rot), _F32)
    zeros_h = jnp.zeros((n, half), _F32)
    zeros_r = jnp.zeros((n, dk - rot), _F32)
    c = jnp.concatenate([cos, cos, ones], axis=1)
    s_hi = jnp.concatenate([-sin, zeros_h, zeros_r], axis=1)
    s_lo = jnp.concatenate([zeros_h, sin, zeros_r], axis=1)
    return tuple(jnp.concatenate([t, t], axis=1) for t in (c, s_hi, s_lo))


def _rotary_body(q_ref, k_ref, c_ref, shi_ref, slo_ref, qo_ref, ko_ref, *, heads, width, half, q_scale):
    c, s_hi, s_lo = c_ref[...], shi_ref[...], slo_ref[...]
    for h in range(heads):
        cols = slice(h * width, (h + 1) * width)
        for src, dst, scale in ((q_ref, qo_ref, q_scale), (k_ref, ko_ref, None)):
            x = src[:, cols]
            y = x * c + pltpu.roll(x, width - half, 1) * s_hi + pltpu.roll(x, half, 1) * s_lo
            if scale is not None:
                y = y * scale
            dst[:, cols] = y.astype(dst.dtype)


def _rotary(proj, tables, *, heads, width, half, q_scale, tr, q_dtype, name):
    M = proj.shape[0]
    W = heads * width
    n_tab = tables[0].shape[0] // tr
    blocks = 2 * (2 * _nbytes((tr, W), _F32) + 3 * _nbytes((tr, width), _F32)
                  + _nbytes((tr, W), q_dtype) + _nbytes((tr, W), _F32))
    tab = pl.BlockSpec((tr, width), lambda i: (i % n_tab, 0))
    return pl.pallas_call(
        functools.partial(_rotary_body, heads=heads, width=width, half=half, q_scale=q_scale),
        grid=(M // tr,),
        in_specs=[pl.BlockSpec((tr, W), lambda i: (i, 0)), pl.BlockSpec((tr, W), lambda i: (i, 1)),
                  tab, tab, tab],
        out_specs=[pl.BlockSpec((tr, W), lambda i: (i, 0)), pl.BlockSpec((tr, W), lambda i: (i, 0))],
        out_shape=[jax.ShapeDtypeStruct((M, W), q_dtype), jax.ShapeDtypeStruct((M, W), _F32)],
        compiler_params=_compiler_params(("parallel",), blocks),
        name=name,
    )(proj, proj, *tables)


def _lambda_value(lq, lambda_init):
    s01 = jnp.sum(lq[0:1, :] * lq[1:2, :], axis=1, keepdims=True)
    s23 = jnp.sum(lq[2:3, :] * lq[3:4, :], axis=1, keepdims=True)
    return jnp.exp(s01) - jnp.exp(s23) + lambda_init


def _sub_rms(x, gain, lambda_init):
    ms = jnp.mean(x * x, axis=-1, keepdims=True)
    return x * lax.rsqrt(ms + EPS) * gain * (1.0 - lambda_init)


def _split_maps(q, dk):
    lane = lax.broadcasted_iota(jnp.int32, q.shape, 1)
    zero = jnp.zeros_like(q)
    return jnp.concatenate([jnp.where(lane < dk, q, zero), jnp.where(lane >= dk, q, zero)], axis=0)


def _prompt_attn_body(lq_ref, sg_ref, q_ref, k_ref, v_ref, *rest, n_casts, tq, dk, lambda_init):
    o_ref = rest[n_casts]
    kb_ref, vt_ref, s_ref, p_ref, acc_ref = rest[2 * n_casts + 1:]
    _Casts.run(rest[:n_casts], rest[n_casts + 1:2 * n_casts + 1])
    qi = pl.program_id(2)
    n_blocks = k_ref.shape[0] // tq

    @pl.when(qi == 0)
    def _stage():
        for blk in range(n_blocks):
            rows = slice(blk * tq, (blk + 1) * tq)
            kb_ref[rows, :] = k_ref[rows, :].astype(_BF16)
            vt_ref[:, rows] = v_ref[rows, :].T.astype(_BF16)

    qt = q_ref[...].astype(_F32).T
    sub = lax.broadcasted_iota(jnp.int32, qt.shape, 0)
    qs = jnp.concatenate([jnp.where(sub < dk, qt, 0.0), jnp.where(sub >= dk, qt, 0.0)], axis=1).astype(_BF16)

    def scores(j):
        k = kb_ref[pl.ds(pl.multiple_of(j * tq, tq), tq), :]
        return jnp.dot(k, qs, preferred_element_type=_F32)

    def weighted_values(j, p):
        vt = vt_ref[:, pl.ds(pl.multiple_of(j * tq, tq), tq)]
        return jnp.dot(vt, p, preferred_element_type=_F32)

    def softmax_update(m, l, slot, masked):
        parts = []
        for c in range(0, 2 * tq, LANES):
            strip = slice(c, c + LANES)
            s = s_ref[slot, :, strip]
            if masked:
                key = lax.broadcasted_iota(jnp.int32, s.shape, 0)
                qry = lax.broadcasted_iota(jnp.int32, s.shape, 1) + (c % tq)
                s = jnp.where(key <= qry, s, -jnp.inf)
            m_new = jnp.maximum(m[:, strip], jnp.max(s, axis=0, keepdims=True))
            a = jnp.exp(m[:, strip] - m_new)
            p = jnp.exp(s - m_new)
            p_ref[slot, :, strip] = p.astype(_BF16)
            parts.append((m_new, a * l[:, strip] + jnp.sum(p, axis=0, keepdims=True), a))
        return tuple(jnp.concatenate(x, axis=1) for x in zip(*parts))

    def trip(j, carry):
        m, l, a_prev = carry
        cur = lax.rem(j, 2)
        pv = weighted_values(jnp.maximum(j - 1, 0), p_ref[1 - cur])
        m, l, a = softmax_update(m, l, cur, False)
        s_ref[1 - cur] = scores(j + 1)
        acc_ref[...] = a_prev * acc_ref[...] + pv
        return m, l, a

    s_ref[0] = scores(0)
    p_ref[1] = jnp.zeros(p_ref.shape[1:], _BF16)
    acc_ref[...] = jnp.zeros(acc_ref.shape, _F32)
    init = (jnp.full((1, 2 * tq), -jnp.inf, _F32), jnp.zeros((1, 2 * tq), _F32), jnp.ones((1, 2 * tq), _F32))
    m, l, a_prev = lax.fori_loop(0, qi, trip, init)
    cur = lax.rem(qi, 2)
    acc = a_prev * acc_ref[...] + weighted_values(jnp.maximum(qi - 1, 0), p_ref[1 - cur])
    m, l, a = softmax_update(m, l, cur, True)
    acc = a * acc + weighted_values(qi, p_ref[cur])
    o = acc / l
    lam = _lambda_value(lq_ref[...], lambda_init)
    out = o[:, :tq] - lam * o[:, tq:]
    ms = jnp.mean(out * out, axis=0, keepdims=True)
    out = out * lax.rsqrt(ms + EPS) * sg_ref[...] * (1.0 - lambda_init)
    o_ref[...] = out.T.astype(o_ref.dtype)


def _prompt_attention(q, k, proj, lq, subln, *, batch, seq, heads, dk, v_col0, lambda_init, tq, casts=(), name):
    M = q.shape[0]
    dv = 2 * dk
    nq = seq // tq
    grid = (batch, heads, nq)
    side = _Casts(casts, grid)
    blocks = 2 * (_nbytes((tq, dv), _BF16) * 2 + 2 * _nbytes((seq, dv), _F32))
    scratch = 6 * _nbytes((2 * tq, tq), _F32)
    outs = pl.pallas_call(
        functools.partial(_prompt_attn_body, n_casts=len(side), tq=tq, dk=dk, lambda_init=lambda_init),
        grid=grid,
        in_specs=[pl.BlockSpec((4, dk), lambda b, h, i: (0, 0)),
                  pl.BlockSpec((dv, 1), lambda b, h, i: (0, 0)),
                  pl.BlockSpec((tq, dv), lambda b, h, i: (b * nq + i, h)),
                  pl.BlockSpec((seq, dv), lambda b, h, i: (b, h)),
                  pl.BlockSpec((seq, dv), lambda b, h, i: (b, v_col0 // dv + h))] + side.in_specs,
        out_specs=[pl.BlockSpec((tq, dv), lambda b, h, i: (b * nq + i, h))] + side.out_specs,
        out_shape=[jax.ShapeDtypeStruct((M, heads * dv), _BF16)] + side.out_shapes,
        scratch_shapes=[pltpu.VMEM((seq, dv), _BF16), pltpu.VMEM((dv, seq), _BF16),
                        pltpu.VMEM((2, tq, 2 * tq), _F32), pltpu.VMEM((2, tq, 2 * tq), _BF16),
                        pltpu.VMEM((dv, 2 * tq), _F32)],
        compiler_params=_compiler_params(("arbitrary", "arbitrary", "arbitrary"), blocks + scratch + side.bytes),
        name=name,
    )(lq, subln.reshape(dv, 1), q, k, proj, *side.views)
    return outs[0], side.finish(outs[1:])


def _decode_attn_body(pt_ref, lq_ref, sg_ref, q_ref, kn_ref, vn_ref, *rest, n_fetch, heads, dk, lambda_init):
    k_refs = rest[:n_fetch]
    v_refs = rest[n_fetch:2 * n_fetch]
    o_ref = rest[2 * n_fetch]
    qm_ref, bias_ref, m_ref, l_ref, acc_ref = rest[2 * n_fetch + 1:]
    j = pl.program_id(1)

    @pl.when(j == 0)
    def _init():
        qm_ref[...] = _split_maps(q_ref[...], dk).astype(_BF16)
        row = lax.broadcasted_iota(jnp.int32, bias_ref.shape, 0)
        col = lax.broadcasted_iota(jnp.int32, bias_ref.shape, 1)
        bias_ref[...] = jnp.where((row & (heads - 1)) == (col & (heads - 1)), 0.0, -jnp.inf)
        m_ref[...] = jnp.full(m_ref.shape, -jnp.inf, _F32)
        l_ref[...] = jnp.zeros(l_ref.shape, _F32)
        acc_ref[...] = jnp.zeros(acc_ref.shape, _F32)

    qm = qm_ref[...]
    for k_ref, v_ref in zip(k_refs, v_refs):
        k = k_ref[...].astype(_BF16)
        v = v_ref[...].astype(_BF16)
        s = lax.dot_general(qm, k, (((1,), (1,)), ((), ())), preferred_element_type=_F32) + bias_ref[...]
        m = m_ref[...]
        m_new = jnp.maximum(m, jnp.max(s, axis=1, keepdims=True))
        a = jnp.exp(m - m_new)
        p = jnp.exp(s - m_new)
        l_ref[...] = a * l_ref[...] + jnp.sum(p, axis=1, keepdims=True)
        acc_ref[...] = a * acc_ref[...] + jnp.dot(p.astype(_BF16), v, preferred_element_type=_F32)
        m_ref[...] = m_new

    @pl.when(j == pl.num_programs(1) - 1)
    def _finish():
        prod = _split_maps(q_ref[...] * kn_ref[...], dk)
        s_self = jnp.sum(prod, axis=1, keepdims=True)
        v_self = jnp.concatenate([vn_ref[...], vn_ref[...]], axis=0)
        m = m_ref[...]
        m_new = jnp.maximum(m, s_self)
        a = jnp.exp(m - m_new)
        p = jnp.exp(s_self - m_new)
        l = a * l_ref[...] + p
        o = (a * acc_ref[...] + p * v_self) / l
        lam = _lambda_value(lq_ref[...], lambda_init)
        out = o[:heads] - lam * o[heads:]
        o_ref[...] = _sub_rms(out, sg_ref[...], lambda_init)


def _decode_attention(q, k_new, v_new, cache_k, cache_v, layer, page_table, lq, subln, *, dk, lambda_init,
                      n_fetch, name):
    B, heads, dv = q.shape
    depth, n_pool, page = cache_k.shape[:3]
    n_pages = page_table.shape[1]
    assert n_pages % n_fetch == 0 and heads & (heads - 1) == 0
    rows = page * heads
    ck = cache_k.reshape(depth, n_pool, rows, dv)
    cv = cache_v.reshape(depth, n_pool, rows, dv)

    def page_spec(r):
        return pl.BlockSpec((None, None, rows, dv),
                            lambda b, j, pt: (layer, pt[b * n_pages + j * n_fetch + r], 0, 0))

    head_spec = pl.BlockSpec((None, heads, dv), lambda b, j, pt: (b, 0, 0))
    blocks = 2 * (2 * n_fetch * _nbytes((rows, dv), _F32) + 4 * _nbytes((heads, dv), _F32))
    scratch = 4 * _nbytes((2 * heads, rows), _F32) + 2 * _nbytes((rows, dv), _BF16)
    grid_spec = pltpu.PrefetchScalarGridSpec(
        num_scalar_prefetch=1,
        grid=(B, n_pages // n_fetch),
        in_specs=[pl.BlockSpec((4, dk), lambda b, j, pt: (0, 0)),
                  pl.BlockSpec((1, dv), lambda b, j, pt: (0, 0)),
                  head_spec, head_spec, head_spec]
                 + [page_spec(r) for r in range(n_fetch)] + [page_spec(r) for r in range(n_fetch)],
        out_specs=head_spec,
        scratch_shapes=[pltpu.VMEM((2 * heads, dv), _BF16), pltpu.VMEM((2 * heads, rows), _F32),
                        pltpu.VMEM((2 * heads, 1), _F32), pltpu.VMEM((2 * heads, 1), _F32),
                        pltpu.VMEM((2 * heads, dv), _F32)],
    )
    return pl.pallas_call(
        functools.partial(_decode_attn_body, n_fetch=n_fetch, heads=heads, dk=dk, lambda_init=lambda_init),
        grid_spec=grid_spec,
        out_shape=jax.ShapeDtypeStruct((B, heads, dv), _F32),
        compiler_params=_compiler_params(("parallel", "arbitrary"), blocks + scratch),
        name=name,
    )(page_table.reshape(-1), lq, subln.reshape(1, dv), q, k_new, v_new,
      *([ck] * n_fetch), *([cv] * n_fetch))


def _hgrn_constants(c):
    t = np.arange(c)
    tt, uu = t[:, None], t[None, :]
    masks = []
    s = c // 2
    while s >= 1:
        x = tt ^ uu
        wide = np.zeros((c, 2 * c), np.float32)
        off = (len(masks) % 2) * c
        wide[:, off:off + c] = (x >= s) & (x < 2 * s) & (tt > uu)
        masks.append(wide)
        s //= 2
    return jnp.asarray(np.stack(masks))


def _level_reference(b, s):
    n = b.shape[0]
    if 2 * s >= 8:
        pieces = [jnp.broadcast_to(b[base + s - 1:base + s, :], (2 * s, b.shape[1])) for base in range(0, n, 2 * s)]
        return jnp.concatenate(pieces, axis=0)
    pos = lax.broadcasted_iota(jnp.int32, b.shape, 0) & (2 * s - 1)
    ref = b
    for p in range(2 * s):
        shift = p - (s - 1)
        if shift != 0:
            ref = jnp.where(pos == p, pltpu.roll(b, shift % n, 0), ref)
    return ref


def _sigmoid(x):
    return 1.0 / (1.0 + jnp.exp(-x))


def _lower_bound(rows, layer):
    top = functools.reduce(jnp.maximum, rows)
    e = [jnp.exp(r - top) for r in rows]
    total = functools.reduce(lambda a, b: a + b, e)
    lb = jnp.zeros_like(total)
    for j in range(1, layer + 1):
        lb = lb + e[j] / total
    return lb


def _hgrn_prompt_body(mask_ref, lbraw_ref, gn_ref, qh_ref, fh_ref, ih_ref, gh_ref, o_ref, s_ref, st_ref, *,
                      layer, chunk, n_chunks):
    n_levels = mask_ref.shape[0]
    width = LANES
    n_heads = st_ref.shape[0]
    lb_all = _lower_bound([lbraw_ref[j:j + 1, :] for j in range(lbraw_ref.shape[0])], layer)
    gn = gn_ref[...]
    row = lax.broadcasted_iota(jnp.int32, (chunk, width), 0)
    st_ref[...] = jnp.zeros(st_ref.shape, _F32)

    def head_chunk(ci, hh):
        rows = pl.ds(pl.multiple_of(ci * chunk, chunk), chunk)
        cols = slice(hh * width, (hh + 1) * width)
        lb = lb_all[:, cols]
        qh, fh, v, gh = qh_ref[rows, cols], fh_ref[rows, cols], ih_ref[rows, cols], gh_ref[rows, cols]
        fg = lb + (1.0 - lb) * _sigmoid(fh)
        q = qh * _sigmoid(qh)
        k = 1.0 - fg
        b = jnp.log(fg)
        d = 1
        while d < chunk:
            b = b + jnp.where(row >= d, pltpu.roll(b, d, 0), 0.0)
            d *= 2
        b_last = b[chunk - 1:chunk]

        def level_operands(s):
            upper = (row & s) != 0
            gap = b - _level_reference(b, s)
            dec = jnp.exp(jnp.where(upper, gap, -gap))
            return jnp.where(upper, q * dec, 0.0), jnp.where(upper, 0.0, k * dec)

        attn = jnp.zeros((chunk, 2 * chunk), _F32)
        for lvl in range(0, n_levels, 2):
            qa, ka = level_operands(chunk >> (lvl + 1))
            qb, kb = level_operands(chunk >> (lvl + 2))
            r = lax.dot_general(jnp.concatenate([qa, qb], axis=0).astype(_BF16),
                                jnp.concatenate([ka, kb], axis=0).astype(_BF16),
                                (((1,), (1,)), ((), ())), preferred_element_type=_F32)
            attn = attn + r[:chunk] * mask_ref[lvl] + r[chunk:] * mask_ref[lvl + 1]
        vt2 = jnp.concatenate([v, v], axis=0).T.astype(_BF16)
        st = st_ref[hh]
        lhs = jnp.concatenate([(q * jnp.exp(b)).astype(_BF16), attn.astype(_BF16)], axis=1)
        rhs = jnp.concatenate([st.astype(_BF16), vt2], axis=1)
        o = lax.dot_general(lhs, rhs, (((1,), (1,)), ((), ())), preferred_element_type=_F32)
        o = o + jnp.sum(q * k, axis=1, keepdims=True) * v
        k_dec = (k * jnp.exp(b_last - b)).astype(_BF16)
        st_ref[hh] = st * jnp.exp(b_last) + jnp.dot(vt2[:, :chunk], k_dec, preferred_element_type=_F32)
        ms = jnp.mean(o * o, axis=-1, keepdims=True)
        o_ref[rows, cols] = (o * lax.rsqrt(ms + EPS) * gn * (gh * _sigmoid(gh))).astype(o_ref.dtype)

    def body(ci, _):
        for hh in range(n_heads):
            head_chunk(ci, hh)
        return 0

    lax.fori_loop(0, n_chunks, body, 0)
    for hh in range(n_heads):
        s_ref[hh] = st_ref[hh].T


def _hgrn_prompt(proj, lb_raw, gnorm, layer, *, batch, seq, heads, col0, heads_per_step, name):
    M = proj.shape[0]
    depth, n_lb = lb_raw.shape
    F = n_lb // heads
    I = gnorm.shape[0]
    hp = heads_per_step
    assert F == LANES and I == LANES and heads % hp == 0 and col0 % (hp * LANES) == 0
    chunk = math.gcd(seq, HGRN_CHUNK)
    masks = _hgrn_constants(chunk)
    assert masks.shape[0] % 2 == 0 and 2 * chunk == LANES
    wide = hp * LANES

    def col(group):
        return pl.BlockSpec((seq, wide), lambda b, h: (b, (col0 + group * heads * LANES) // wide + h))

    blocks = 2 * (4 * _nbytes((seq, wide), _F32) + _nbytes((seq, wide), _BF16)
                  + _nbytes(masks.shape, _F32) + _nbytes((hp, F, I), _F32))
    return pl.pallas_call(
        functools.partial(_hgrn_prompt_body, layer=layer, chunk=chunk, n_chunks=seq // chunk),
        grid=(batch, heads // hp),
        in_specs=[pl.BlockSpec(masks.shape, lambda b, h: (0, 0, 0)),
                  pl.BlockSpec((depth, wide), lambda b, h: (0, h)),
                  pl.BlockSpec((1, I), lambda b, h: (0, 0)),
                  col(0), col(1), col(2), col(3)],
        out_specs=[pl.BlockSpec((seq, wide), lambda b, h: (b, h)),
                   pl.BlockSpec((None, hp, F, I), lambda b, h: (b, h, 0, 0))],
        out_shape=[jax.ShapeDtypeStruct((M, heads * I), _BF16),
                   jax.ShapeDtypeStruct((batch, heads, F, I), _F32)],
        scratch_shapes=[pltpu.VMEM((hp, I, F), _F32)],
        compiler_params=_compiler_params(("parallel", "parallel"), blocks + (4 << 20)),
        name=name,
    )(masks, lb_raw, gnorm.reshape(1, I), proj, proj, proj, proj)


def _hgrn_step_body(lbraw_ref, gn_ref, qh_ref, fh_ref, ih_ref, gh_ref, s0_ref, o_ref, s_ref, *, layer, heads):
    F = fh_ref.shape[1]
    lb = _lower_bound([lbraw_ref[j] for j in range(lbraw_ref.shape[0])], layer)
    fg = lb + (1.0 - lb) * _sigmoid(fh_ref[...])
    qh = qh_ref[...]
    q = qh * _sigmoid(qh)
    k = 1.0 - fg
    pad = jnp.zeros((F - 3 * heads, F), _F32)
    cols = jnp.concatenate([fg, k, q, pad], axis=0).T
    v = ih_ref[...]
    outs = []
    for h in range(heads):
        s_new = cols[:, h:h + 1] * s0_ref[h] + cols[:, heads + h:heads + h + 1] * v[h:h + 1, :]
        s_ref[h] = s_new
        outs.append(jnp.sum(cols[:, 2 * heads + h:2 * heads + h + 1] * s_new, axis=0, keepdims=True))
    o = jnp.concatenate(outs, axis=0)
    gh = gh_ref[...]
    ms = jnp.mean(o * o, axis=-1, keepdims=True)
    o_ref[...] = o * lax.rsqrt(ms + EPS) * gn_ref[...] * (gh * _sigmoid(gh))


def _hgrn_step(proj3, lb_raw, gnorm, state, layer, *, heads, group0, name):
    B = proj3.shape[0]
    depth, n_lb = lb_raw.shape
    F, I = state.shape[3], state.shape[4]
    assert F == LANES and I == LANES and 3 * heads <= F

    def grp(g):
        return pl.BlockSpec((None, heads, LANES), lambda b: (b, group0 + g, 0))

    blocks = 2 * (2 * _nbytes((heads, F, I), _F32) + 6 * _nbytes((heads, LANES), _F32))
    return pl.pallas_call(
        functools.partial(_hgrn_step_body, layer=layer, heads=heads),
        grid=(B,),
        in_specs=[pl.BlockSpec((depth, heads, F), lambda b: (0, 0, 0)),
                  pl.BlockSpec((1, I), lambda b: (0, 0)),
                  grp(0), grp(1), grp(2), grp(3),
                  pl.BlockSpec((None, None, heads, F, I), lambda b: (layer, b, 0, 0, 0))],
        out_specs=[pl.BlockSpec((None, heads, I), lambda b: (b, 0, 0)),
                   pl.BlockSpec((None, heads, F, I), lambda b: (b, 0, 0, 0))],
        out_shape=[jax.ShapeDtypeStruct((B, heads, I), _F32),
                   jax.ShapeDtypeStruct((B, heads, F, I), _F32)],
        compiler_params=_compiler_params(("parallel",), blocks + (2 << 20)),
        name=name,
    )(lb_raw.reshape(depth, heads, F), gnorm.reshape(1, I), proj3, proj3, proj3, proj3, state)


def _pick(n, candidates):
    for c in candidates:
        if n % c == 0:
            return c
    return n


def _tiles(M, d_model, d_ff, in_cols):
    tm = _pick(M, (1024, 512, 256, 128, 64, 32, 16))
    return dict(
        tm=tm,
        tm_up=_pick(M, (2048, 1024, 512, 256, 128, 64, 32, 16)),
        tn_up=_pick(d_ff, (256, 128)),
        tm_down=min(tm, 512),
        tn_down=_pick(d_model, (512, 256, 128)),
        tn_in=_pick(in_cols, (1024, 512, 256, 128)),
        tn_out=_pick(d_model, (1024, 512, 256, 128)),
        tr=_pick(M, (256, 128, 64, 32, 16)),
    )


def _decode_tiles(m_s, d_model, d_ff, in_cols):
    return dict(tm=m_s, tn_up=_pick(d_ff, (256, 128)), tn_down=_pick(d_model, (512, 256, 128)),
                tn_in=_pick(in_cols, (1024, 512, 256, 128)), tn_out=_pick(d_model, (1024, 512, 256, 128)))


def _ffn_blocks(xp, xpb, xs, xsb, wg_b, wu_b, wd_cast, gain, bias, alpha, t_p, t_s, tag, *, h_s=None,
                up_casts=(), down_casts=()):
    h_p, cast = _ffn_up(xpb, wg_b, wu_b, tm=t_p["tm_up"], tn=t_p["tn_up"], casts=[wd_cast, *up_casts],
                        name=f"ffn_up_p{tag}")
    wd_b = cast[0]
    if h_s is None:
        h_s, _ = _ffn_up(xsb, wg_b, wu_b, tm=t_s["tm"], tn=t_s["tn_up"], name=f"ffn_up_s{tag}")
    a_s, _ = _matmul(h_s, wd_b, tm=t_s["tm"], tn=t_s["tn_down"], name=f"ffn_down_s{tag}")
    xs, xsb = _residual_ln(xs, a_s, gain, bias, alpha=alpha, scale=0.5, tr=t_s["tm"], name=f"ln_s{tag}")
    a_p, down_cast = _matmul(h_p, wd_b, tm=t_p["tm_down"], tn=t_p["tn_down"], casts=down_casts,
                             name=f"ffn_down_p{tag}")
    xp, xpb = _residual_ln(xp, a_p, gain, bias, alpha=alpha, scale=0.5, tr=t_p["tr"], name=f"ln_p{tag}")
    return xp, xpb, xs, xsb, cast[1:], down_cast


def kernel(x_prompt, x_sample, cache_k, cache_v, state_hgrn, page_table, ln_gain, ln_bias, ffn1_gate, ffn1_up, ffn1_down, w_in, lambda_qk, subln_gain, hgrn_lower_bound, hgrn_gnorm_gain, w_out, ffn2_gate, ffn2_up, ffn2_down):
    batch, seq, d_model = x_prompt.shape
    dec_batch, dec_seq, _ = x_sample.shape
    assert dec_seq == 1
    depth = w_in.shape[0]
    att_heads, dv = cache_v.shape[3], cache_v.shape[4]
    dk = dv // 2
    assert cache_k.shape[4] == dv == LANES
    rot = dk // 4
    hg_heads, hg_f, hg_i = state_hgrn.shape[2:]
    d_ff = ffn1_gate.shape[2]
    in_cols = w_in.shape[2]
    att_width = att_heads * dv
    hg_col0 = 3 * att_width
    alpha = (2 * depth) ** 0.25
    att_scale = dk ** -0.5
    n_past = page_table.shape[1] * cache_k.shape[2]

    m_p = batch * seq
    m_s = 16
    t_p = _tiles(m_p, d_model, d_ff, in_cols)
    t_s = _decode_tiles(m_s, d_model, d_ff, in_cols)
    tab_p = _rotary_tables(jnp.arange(seq, dtype=jnp.int32), dk, rot)
    tab_s = _rotary_tables(jnp.full((m_s,), n_past, jnp.int32), dk, rot)
    tq = _pick(seq, (256, 128))

    xp = x_prompt.reshape(m_p, d_model)
    xs = jnp.pad(x_sample.reshape(dec_batch, d_model), ((0, m_s - dec_batch), (0, 0)))
    xpb, xsb = xp.astype(_BF16), xs.astype(_BF16)

    h_s0, wg_b, wu_b = _ffn_up_cast(xsb, ffn1_gate, ffn1_up, 0, tn=t_s["tn_up"], name="ffn_up_s0a")

    outs = {k: [] for k in ("kp", "vp", "sp", "ks", "vs", "ss")}
    for l in range(depth):
        lambda_init = 0.8 - 0.6 * math.exp(-0.3 * l)
        gains, biases = ln_gain[l], ln_bias[l]

        xp, xpb, xs, xsb, (win_b,), (wout_b,) = _ffn_blocks(
            xp, xpb, xs, xsb, wg_b, wu_b, (ffn1_down, l), gains[0], biases[0], alpha, t_p, t_s, f"{l}a",
            h_s=h_s0 if l == 0 else None, up_casts=[(w_in, l)], down_casts=[(w_out, l)])

        proj_s, _ = _matmul(xsb, win_b, tm=t_s["tm"], tn=t_s["tn_in"], name=f"w_in_s{l}")
        q_s, k_s = _rotary(proj_s, tab_s, heads=att_heads, width=dv, half=rot // 2, q_scale=att_scale,
                           tr=m_s, q_dtype=_F32, name=f"rotary_s{l}")
        proj3 = proj_s[:dec_batch].reshape(dec_batch, in_cols // LANES, LANES)
        q3 = q_s[:dec_batch].reshape(dec_batch, att_heads, dv)
        k3 = k_s[:dec_batch].reshape(dec_batch, att_heads, dv)
        v3 = proj_s[:dec_batch, 2 * att_width:3 * att_width].reshape(dec_batch, att_heads, dv)
        att_s = _decode_attention(q3, k3, v3, cache_k, cache_v, l, page_table, lambda_qk[l], subln_gain[l],
                                  dk=dk, lambda_init=lambda_init, n_fetch=_pick(page_table.shape[1], (8, 4, 2, 1)),
                                  name=f"attn_s{l}")
        o_s, st_s = _hgrn_step(proj3, hgrn_lower_bound, hgrn_gnorm_gain[l], state_hgrn, l, heads=hg_heads,
                               group0=hg_col0 // (hg_heads * LANES), name=f"hgrn_s{l}")
        merged_s = jnp.concatenate([att_s.reshape(dec_batch, att_width), o_s.reshape(dec_batch, hg_heads * hg_i)],
                                   axis=1)
        merged_s = jnp.pad(merged_s, ((0, m_s - dec_batch), (0, 0))).astype(_BF16)
        mixed_s, _ = _matmul(merged_s, wout_b, tm=t_s["tm"], tn=t_s["tn_out"], name=f"w_out_s{l}")
        xs, xsb = _residual_ln(xs, mixed_s, gains[1], biases[1], alpha=alpha, scale=1.0, tr=m_s,
                               name=f"ln_s{l}b")
        outs["ks"].append(k3.reshape(dec_batch, 1, att_heads, dv))
        outs["vs"].append(v3.reshape(dec_batch, 1, att_heads, dv))
        outs["ss"].append(st_s)

        proj, _ = _matmul(xpb, win_b, tm=t_p["tm"], tn=t_p["tn_in"], name=f"w_in_p{l}")
        q_rot, k_rot = _rotary(proj, tab_p, heads=att_heads, width=dv, half=rot // 2, q_scale=att_scale,
                               tr=min(t_p["tr"], seq), q_dtype=_BF16, name=f"rotary_p{l}")
        att, (wg_c, wu_c) = _prompt_attention(
            q_rot, k_rot, proj, lambda_qk[l], subln_gain[l], batch=batch, seq=seq, heads=att_heads, dk=dk,
            v_col0=2 * att_width, lambda_init=lambda_init, tq=tq, casts=[(ffn2_gate, l), (ffn2_up, l)],
            name=f"attn_p{l}")
        o_h, s_new = _hgrn_prompt(proj, hgrn_lower_bound, hgrn_gnorm_gain[l], l, batch=batch, seq=seq,
                                  heads=hg_heads, col0=hg_col0, heads_per_step=_pick(hg_heads, (4, 2, 1)),
                                  name=f"hgrn_p{l}")
        mixed = _matmul_concat(att, o_h, wout_b, tm=t_p["tm"], tn=t_p["tn_out"], name=f"w_out_p{l}")
        xp, xpb = _residual_ln(xp, mixed, gains[1], biases[1], alpha=alpha, scale=1.0, tr=t_p["tr"],
                               name=f"ln_p{l}b")
        outs["kp"].append(k_rot.reshape(batch, seq, att_heads, dv))
        outs["vp"].append(proj[:, 2 * att_width:3 * att_width].reshape(batch, seq, att_heads, dv))
        outs["sp"].append(s_new)

        ahead = [(ffn1_gate, l + 1), (ffn1_up, l + 1)] if l + 1 < depth else []
        xp, xpb, xs, xsb, nxt, _ = _ffn_blocks(
            xp, xpb, xs, xsb, wg_c, wu_c, (ffn2_down, l), gains[2], biases[2], alpha, t_p, t_s, f"{l}c",
            up_casts=ahead)
        if ahead:
            wg_b, wu_b = nxt

    return (xp.reshape(batch, seq, d_model), xs[:dec_batch].reshape(dec_batch, 1, d_model),
            jnp.stack(outs["kp"]), jnp.stack(outs["vp"]), jnp.stack(outs["sp"]),
            jnp.stack(outs["ks"]), jnp.stack(outs["vs"]), jnp.stack(outs["ss"]))
```
